```python
import math
import jax
import jax.numpy as jnp
from jax import lax
import numpy as np

D_MODEL = 1024
BATCH = 4
SEQ = 4096
DEPTH = 2
DEC_BATCH = 128
DEC_SEQ = 1
PAST_LEN = 2048
PAGE_SIZE = 128

N_EVEN = (DEPTH + 1) // 2
N_ODD = DEPTH // 2
FOX_HEADS = 8
FOX_HD = 64
FOX_W = FOX_HEADS * FOX_HD
SGU_GROUPS = 4
SGU_CH = 128
SGU_W = SGU_GROUPS * SGU_CH
CHUNK = 128
DIFF_HEADS = 8
DIFF_HD = 64
DIFF_W = DIFF_HEADS * 2 * DIFF_HD
IN_EVEN = 3 * FOX_W + FOX_HEADS + 2 * SGU_W
EVEN_SPLITS = [FOX_W, 2 * FOX_W, 3 * FOX_W, 3 * FOX_W + FOX_HEADS, 3 * FOX_W + FOX_HEADS + SGU_W]
OUT_EVEN = FOX_W + SGU_W
IN_ODD = 3 * DIFF_W
D_FF = 2816
N_EXPERTS = 8
TOP_K = 2
D_FF_EXPERT = 3584
Q_BLOCK = 128
NORM_EPS = 1e-6
FORGET_BIAS_INIT = 3.0
FOX_SCALE = FOX_HD ** -0.5
DIFF_SCALE = DIFF_HD ** -0.5

kernel_name = 'hybrid_fox_sgu_diffattn_moe_step'

F32 = jnp.float32


def rms_norm(x, g):
    xf = x.astype(F32)
    y = xf * lax.rsqrt(jnp.mean(xf * xf, axis=-1, keepdims=True) + NORM_EPS)
    return (y * g.astype(F32)).astype(x.dtype)


def layer_norm(x, g, b):
    xf = x.astype(F32)
    xc = xf - jnp.mean(xf, axis=-1, keepdims=True)
    y = xc * lax.rsqrt(jnp.mean(xc * xc, axis=-1, keepdims=True) + NORM_EPS)
    return (y * g.astype(F32) + b.astype(F32)).astype(x.dtype)


def causal_mask(n_q, n_k, q_offset):
    return jnp.arange(n_k)[None, :] <= (q_offset + jnp.arange(n_q))[:, None]


def gather_pages(pool, page_table):
    g = pool[page_table]
    return g.reshape((g.shape[0], g.shape[1] * g.shape[2]) + g.shape[3:])


def two_part_softmax(s_past, s_new):
    n_past = s_past.shape[-1]
    p = jax.nn.softmax(jnp.concatenate([s_past, s_new], axis=-1), axis=-1)
    return p[..., :n_past], p[..., n_past:]


def even_in_proj(x, norm_g, w_in, b_in, q_gain, k_gain):
    b, s = x.shape[:2]
    z = jnp.einsum('bsd,de->bse', rms_norm(x, norm_g), w_in) + b_in
    q, k, v, f, u, vg = jnp.split(z, EVEN_SPLITS, axis=-1)
    q = rms_norm(q.reshape(b, s, FOX_HEADS, FOX_HD), q_gain)
    k = rms_norm(k.reshape(b, s, FOX_HEADS, FOX_HD), k_gain)
    v = v.reshape(b, s, FOX_HEADS, FOX_HD)
    logf = jax.nn.log_sigmoid(f.astype(F32))
    u = jax.nn.gelu(u, approximate=False)
    vg = jax.nn.gelu(vg, approximate=False)
    return q, k, v, logf, u, vg


def fox_prompt(q, k, v, logf):
    b, s, h, d = q.shape
    nb = s // Q_BLOCK
    c = jnp.cumsum(logf, axis=1)
    c_t = jnp.swapaxes(c, 1, 2)
    q_blocks = jnp.moveaxis(q.reshape(b, nb, Q_BLOCK, h, d), 1, 0)
    c_blocks = jnp.moveaxis(c.reshape(b, nb, Q_BLOCK, h), 1, 0)

    def block(args):
        qi, ci, i = args
        sc = jnp.einsum('bqhd,bkhd->bhqk', qi, k).astype(F32) * FOX_SCALE
        bias = jnp.swapaxes(ci, 1, 2)[..., :, None] - c_t[..., None, :]
        sc = jnp.where(causal_mask(Q_BLOCK, s, i * Q_BLOCK), sc + bias, -jnp.inf)
        p = jax.nn.softmax(sc, axis=-1)
        return jnp.einsum('bhqk,bkhd->bqhd', p.astype(v.dtype), v)

    o = lax.map(block, (q_blocks, c_blocks, jnp.arange(nb)))
    return jnp.moveaxis(o, 0, 1).reshape(b, s, h, d)


def fox_sample(q, k_new, v_new, logf_new, k_past, v_past, logf_past):
    n_new = q.shape[1]
    cum_past = jnp.cumsum(logf_past.astype(F32), axis=1)
    suffix = jnp.swapaxes(cum_past[:, -1:, :] - cum_past, 1, 2)
    cum_new = jnp.swapaxes(jnp.cumsum(logf_new, axis=1), 1, 2)
    bias_past = cum_new[..., :, None] + suffix[..., None, :]
    bias_new = cum_new[..., :, None] - cum_new[..., None, :]
    s_past = jnp.einsum('bqhd,bkhd->bhqk', q, k_past).astype(F32) * FOX_SCALE + bias_past
    s_new = jnp.einsum('bqhd,bkhd->bhqk', q, k_new).astype(F32) * FOX_SCALE + bias_new
    s_new = jnp.where(causal_mask(n_new, n_new, 0), s_new, -jnp.inf)
    p_past, p_new = two_part_softmax(s_past, s_new)
    return (jnp.einsum('bhqk,bkhd->bqhd', p_past.astype(v_past.dtype), v_past)
            + jnp.einsum('bhqk,bkhd->bqhd', p_new.astype(v_new.dtype), v_new))


def sgu(u, vg, ln_g, ln_b, w_s, b_s):
    b, s = u.shape[:2]
    n = min(s, CHUNK)
    vn = layer_norm(vg.reshape(b, s, SGU_GROUPS, SGU_CH), ln_g, ln_b)
    w = jnp.tril(w_s[:, :n, :n])
    vc = vn.reshape(b, s // n, n, SGU_GROUPS, SGU_CH)
    mixed = jnp.einsum('gij,bcjgd->bcigd', w, vc) + jnp.swapaxes(b_s[:, :n], 0, 1)[None, None, :, :, None]
    out = u.reshape(b, s // n, n, SGU_GROUPS, SGU_CH) * mixed
    return out.reshape(b, s, SGU_W), vn


def even_out(fox_o, sgu_o, w_out):
    b, s = sgu_o.shape[:2]
    cat = jnp.concatenate([fox_o.reshape(b, s, FOX_W), sgu_o], axis=-1)
    return jnp.einsum('bse,ed->bsd', cat, w_out)


def swiglu(x, w_gate, w_up, w_down):
    return (jax.nn.silu(x @ w_gate) * (x @ w_up)) @ w_down


def odd_in_proj(x, norm_g, w_in, q_gain, k_gain):
    b, s = x.shape[:2]
    z = jnp.einsum('bsd,de->bse', rms_norm(x, norm_g), w_in)
    q, k, v = jnp.split(z, 3, axis=-1)
    q = rms_norm(q.reshape(b, s, DIFF_HEADS, 2, DIFF_HD), q_gain)
    k = rms_norm(k.reshape(b, s, DIFF_HEADS, 2, DIFF_HD), k_gain)
    v = v.reshape(b, s, DIFF_HEADS, 2 * DIFF_HD)
    return q, k, v


def diff_lambda(lq1, lk1, lq2, lk2, lam_init):
    return (jnp.exp(jnp.sum(lq1.astype(F32) * lk1.astype(F32)))
            - jnp.exp(jnp.sum(lq2.astype(F32) * lk2.astype(F32))) + lam_init)


def diff_prompt(q, k, v, lam):
    b, s, h = q.shape[:3]
    nb = s // Q_BLOCK
    q_blocks = jnp.moveaxis(q.reshape(b, nb, Q_BLOCK, h, 2, DIFF_HD), 1, 0)

    def block(args):
        qi, i = args
        sc = jnp.einsum('bqhcd,bkhcd->bhcqk', qi, k).astype(F32) * DIFF_SCALE
        sc = jnp.where(causal_mask(Q_BLOCK, s, i * Q_BLOCK), sc, -jnp.inf)
        p = jax.nn.softmax(sc, axis=-1)
        w = p[:, :, 0] - lam * p[:, :, 1]
        return jnp.einsum('bhqk,bkhe->bqhe', w.astype(v.dtype), v)

    o = lax.map(block, (q_blocks, jnp.arange(nb)))
    return jnp.moveaxis(o, 0, 1).reshape(b, s, h, 2 * DIFF_HD)


def diff_sample(q, k_new, v_new, k_past, v_past, lam):
    n_new = q.shape[1]
    s_past = jnp.einsum('bqhcd,bkhcd->bhcqk', q, k_past).astype(F32) * DIFF_SCALE
    s_new = jnp.einsum('bqhcd,bkhcd->bhcqk', q, k_new).astype(F32) * DIFF_SCALE
    s_new = jnp.where(causal_mask(n_new, n_new, 0), s_new, -jnp.inf)
    p_past, p_new = two_part_softmax(s_past, s_new)
    w_past = p_past[:, :, 0] - lam * p_past[:, :, 1]
    w_new = p_new[:, :, 0] - lam * p_new[:, :, 1]
    return (jnp.einsum('bhqk,bkhe->bqhe', w_past.astype(v_past.dtype), v_past)
            + jnp.einsum('bhqk,bkhe->bqhe', w_new.astype(v_new.dtype), v_new))


def diff_out(o, subln_g, lam_init, w_out):
    b, s = o.shape[:2]
    o = rms_norm(o, subln_g) * (1.0 - lam_init)
    return jnp.einsum('bse,ed->bsd', o.reshape(b, s, DIFF_W), w_out)


def moe_swiglu(x, router_w, router_b, w_gate, w_up, w_down):
    logits = (x @ router_w).astype(F32) + router_b.astype(F32)
    top_v, top_i = lax.top_k(logits, TOP_K)
    gates = jax.nn.softmax(top_v, axis=-1)
    combine = jnp.sum(jax.nn.one_hot(top_i, N_EXPERTS, dtype=F32) * gates[..., None], axis=-2)
    combine = combine.astype(x.dtype)
    y = jnp.zeros_like(x)
    for e in range(N_EXPERTS):
        y = y + combine[..., e:e + 1] * swiglu(x, w_gate[e], w_up[e], w_down[e])
    return y


def setup_inputs(seed: int = 0) -> dict:
    key = jax.random.key(seed)
    keys = iter(jax.random.split(key, 48))

    def normal(shape, scale):
        return scale * jax.random.normal(next(keys), shape, F32)

    def gain(shape):
        return 1.0 + 0.1 * jax.random.normal(next(keys), shape, F32)

    n_pages = PAST_LEN // PAGE_SIZE
    n_used = DEC_BATCH * n_pages
    n_phys = n_used + max(1, n_used // 4)

    x_prompt = normal((BATCH, SEQ, D_MODEL), 1.0)
    x_sample = normal((DEC_BATCH, DEC_SEQ, D_MODEL), 1.0)
    cache_fox_k = normal((N_EVEN, n_phys, PAGE_SIZE, FOX_HEADS, FOX_HD), 1.0)
    cache_fox_v = normal((N_EVEN, n_phys, PAGE_SIZE, FOX_HEADS, FOX_HD), 1.0)
    cache_fox_logf = jax.nn.log_sigmoid(normal((N_EVEN, n_phys, PAGE_SIZE, FOX_HEADS), 1.0) + FORGET_BIAS_INIT)
    cache_diff_k = normal((N_ODD, n_phys, PAGE_SIZE, DIFF_HEADS, 2, DIFF_HD), 1.0)
    cache_diff_v = normal((N_ODD, n_phys, PAGE_SIZE, DIFF_HEADS, 2 * DIFF_HD), 1.0)
    page_table = jax.random.permutation(next(keys), n_phys)[:n_used].reshape(DEC_BATCH, n_pages).astype(jnp.int32)

    even_b_in = normal((N_EVEN, IN_EVEN), 0.02)
    even_b_in = even_b_in.at[:, 3 * FOX_W:3 * FOX_W + FOX_HEADS].add(FORGET_BIAS_INIT)

    return {
        'x_prompt': x_prompt,
        'x_sample': x_sample,
        'cache_fox_k': cache_fox_k,
        'cache_fox_v': cache_fox_v,
        'cache_fox_logf': cache_fox_logf,
        'cache_diff_k': cache_diff_k,
        'cache_diff_v': cache_diff_v,
        'page_table': page_table,
        'even_norm_mix': gain((N_EVEN, D_MODEL)),
        'even_w_in': normal((N_EVEN, D_MODEL, IN_EVEN), D_MODEL ** -0.5),
        'even_b_in': even_b_in,
        'fox_q_gain': gain((N_EVEN, FOX_HD)),
        'fox_k_gain': gain((N_EVEN, FOX_HD)),
        'sgu_ln_gain': gain((N_EVEN, SGU_GROUPS, SGU_CH)),
        'sgu_ln_bias': normal((N_EVEN, SGU_GROUPS, SGU_CH), 0.02),
        'sgu_w_s': normal((N_EVEN, SGU_GROUPS, CHUNK, CHUNK), CHUNK ** -0.5),
        'sgu_b_s': gain((N_EVEN, SGU_GROUPS, CHUNK)),
        'even_w_out': normal((N_EVEN, OUT_EVEN, D_MODEL), OUT_EVEN ** -0.5),
        'even_norm_ffn': gain((N_EVEN, D_MODEL)),
        'ffn_w_gate': normal((N_EVEN, D_MODEL, D_FF), D_MODEL ** -0.5),
        'ffn_w_up': normal((N_EVEN, D_MODEL, D_FF), D_MODEL ** -0.5),
        'ffn_w_down': normal((N_EVEN, D_FF, D_MODEL), D_FF ** -0.5),
        'odd_norm_mix': gain((N_ODD, D_MODEL)),
        'odd_w_in': normal((N_ODD, D_MODEL, IN_ODD), D_MODEL ** -0.5),
        'diff_q_gain': gain((N_ODD, 2, DIFF_HD)),
        'diff_k_gain': gain((N_ODD, 2, DIFF_HD)),
        'diff_lambda_q1': normal((N_ODD, DIFF_HD), 0.1),
        'diff_lambda_k1': normal((N_ODD, DIFF_HD), 0.1),
        'diff_lambda_q2': normal((N_ODD, DIFF_HD), 0.1),
        'diff_lambda_k2': normal((N_ODD, DIFF_HD), 0.1),
        'diff_subln_gain': gain((N_ODD, 2 * DIFF_HD)),
        'odd_w_out': normal((N_ODD, DIFF_W, D_MODEL), DIFF_W ** -0.5),
        'odd_norm_ffn': gain((N_ODD, D_MODEL)),
        'router_w': normal((N_ODD, D_MODEL, N_EXPERTS), D_MODEL ** -0.5),
        'router_b': normal((N_ODD, N_EXPERTS), 0.01),
        'moe_w_gate': normal((N_ODD, N_EXPERTS, D_MODEL, D_FF_EXPERT), D_MODEL ** -0.5),
        'moe_w_up': normal((N_ODD, N_EXPERTS, D_MODEL, D_FF_EXPERT), D_MODEL ** -0.5),
        'moe_w_down': normal((N_ODD, N_EXPERTS, D_FF_EXPERT, D_MODEL), D_FF_EXPERT ** -0.5),
    }


def reference(x_prompt, x_sample, cache_fox_k, cache_fox_v, cache_fox_logf, cache_diff_k, cache_diff_v, page_table,
              even_norm_mix, even_w_in, even_b_in, fox_q_gain, fox_k_gain, sgu_ln_gain, sgu_ln_bias, sgu_w_s, sgu_b_s,
              even_w_out, even_norm_ffn, ffn_w_gate, ffn_w_up, ffn_w_down,
              odd_norm_mix, odd_w_in, diff_q_gain, diff_k_gain, diff_lambda_q1, diff_lambda_k1, diff_lambda_q2,
              diff_lambda_k2, diff_subln_gain, odd_w_out, odd_norm_ffn, router_w, router_b,
              moe_w_gate, moe_w_up, moe_w_down):
    yp, ys = x_prompt, x_sample
    fox_k_p, fox_v_p, fox_lf_p, diff_k_p, diff_v_p = [], [], [], [], []
    fox_k_s, fox_v_s, fox_lf_s, diff_k_s, diff_v_s, sgu_v_s = [], [], [], [], [], []

    for layer in range(DEPTH):
        i = layer // 2
        if layer % 2 == 0:
            q, k, v, lf, u, vg = even_in_proj(yp, even_norm_mix[i], even_w_in[i], even_b_in[i], fox_q_gain[i], fox_k_gain[i])
            fo = fox_prompt(q, k, v, lf)
            so, _ = sgu(u, vg, sgu_ln_gain[i], sgu_ln_bias[i], sgu_w_s[i], sgu_b_s[i])
            yp = yp + even_out(fo, so, even_w_out[i])
            fox_k_p.append(k)
            fox_v_p.append(v)
            fox_lf_p.append(lf)
            q, k, v, lf, u, vg = even_in_proj(ys, even_norm_mix[i], even_w_in[i], even_b_in[i], fox_q_gain[i], fox_k_gain[i])
            fo = fox_sample(q, k, v, lf,
                            gather_pages(cache_fox_k[i], page_table),
                            gather_pages(cache_fox_v[i], page_table),
                            gather_pages(cache_fox_logf[i], page_table))
            so, vn = sgu(u, vg, sgu_ln_gain[i], sgu_ln_bias[i], sgu_w_s[i], sgu_b_s[i])
            ys = ys + even_out(fo, so, even_w_out[i])
            fox_k_s.append(k)
            fox_v_s.append(v)
            fox_lf_s.append(lf)
            sgu_v_s.append(vn)
            yp = yp + swiglu(rms_norm(yp, even_norm_ffn[i]), ffn_w_gate[i], ffn_w_up[i], ffn_w_down[i])
            ys = ys + swiglu(rms_norm(ys, even_norm_ffn[i]), ffn_w_gate[i], ffn_w_up[i], ffn_w_down[i])
        else:
            lam_init = 0.8 - 0.6 * math.exp(-0.3 * layer)
            lam = diff_lambda(diff_lambda_q1[i], diff_lambda_k1[i], diff_lambda_q2[i], diff_lambda_k2[i], lam_init)
            q, k, v = odd_in_proj(yp, odd_norm_mix[i], odd_w_in[i], diff_q_gain[i], diff_k_gain[i])
            yp = yp + diff_out(diff_prompt(q, k, v, lam), diff_subln_gain[i], lam_init, odd_w_out[i])
            diff_k_p.append(k)
            diff_v_p.append(v)
            q, k, v = odd_in_proj(ys, odd_norm_mix[i], odd_w_in[i], diff_q_gain[i], diff_k_gain[i])
            o = diff_sample(q, k, v,
                            gather_pages(cache_diff_k[i], page_table),
                            gather_pages(cache_diff_v[i], page_table), lam)
            ys = ys + diff_out(o, diff_subln_gain[i], lam_init, odd_w_out[i])
            diff_k_s.append(k)
            diff_v_s.append(v)
            yp = yp + moe_swiglu(rms_norm(yp, odd_norm_ffn[i]), router_w[i], router_b[i], moe_w_gate[i], moe_w_up[i], moe_w_down[i])
            ys = ys + moe_swiglu(rms_norm(ys, odd_norm_ffn[i]), router_w[i], router_b[i], moe_w_gate[i], moe_w_up[i], moe_w_down[i])

    new_fox_k_prompt = jnp.stack(fox_k_p, axis=0)
    new_fox_v_prompt = jnp.stack(fox_v_p, axis=0)
    new_fox_logf_prompt = jnp.stack(fox_lf_p, axis=0)
    new_diff_k_prompt = jnp.stack(diff_k_p, axis=0)
    new_diff_v_prompt = jnp.stack(diff_v_p, axis=0)
    new_fox_k_sample = jnp.stack(fox_k_s, axis=0)
    new_fox_v_sample = jnp.stack(fox_v_s, axis=0)
    new_fox_logf_sample = jnp.stack(fox_lf_s, axis=0)
    new_diff_k_sample = jnp.stack(diff_k_s, axis=0)
    new_diff_v_sample = jnp.stack(diff_v_s, axis=0)
    new_sgu_v_sample = jnp.stack(sgu_v_s, axis=0)
    return (yp, ys,
            new_fox_k_prompt, new_fox_v_prompt, new_fox_logf_prompt, new_diff_k_prompt, new_diff_v_prompt,
            new_fox_k_sample, new_fox_v_sample, new_fox_logf_sample, new_diff_k_sample, new_diff_v_sample,
            new_sgu_v_sample)
```

```python
import functools
import math

import jax
import jax.numpy as jnp
from jax import lax
from jax.experimental import pallas as pl
from jax.experimental.pallas import tpu as pltpu

F32 = jnp.float32
BF16 = jnp.bfloat16

D_MODEL = 1024
FOX_HEADS = 8
FOX_HD = 64
FOX_W = 512
SGU_GROUPS = 4
SGU_CH = 128
SGU_W = 512
CHUNK = 128
DIFF_HEADS = 8
DIFF_HD = 64
DIFF_W = 1024
N_EXPERTS = 8
PAGE_SIZE = 128
NORM_EPS = 1e-6
FOX_SCALE = FOX_HD ** -0.5
DIFF_SCALE = DIFF_HD ** -0.5
LAM_INIT = 0.8 - 0.6 * math.exp(-0.3 * 1)

LANES = 128
MXU_DIM = 256
VMEM_LIMIT = 52 * 1024 * 1024


def _cp(n_axes, vmem=VMEM_LIMIT):
    return pltpu.CompilerParams(dimension_semantics=("arbitrary",) * n_axes, vmem_limit_bytes=vmem)


def _dot(a, b):
    return jnp.dot(a, b, preferred_element_type=F32)


def _dot_nt(a, b):
    return lax.dot_general(a, b, (((1,), (1,)), ((), ())), preferred_element_type=F32)


def _dot_split3(x, m_bf16):
    hi = x.astype(BF16)
    r1 = x - hi.astype(F32)
    mid = r1.astype(BF16)
    lo = (r1 - mid.astype(F32)).astype(BF16)
    return _dot(hi, m_bf16) + _dot(mid, m_bf16) + _dot(lo, m_bf16)


def _rms(x):
    return x * lax.rsqrt(jnp.mean(x * x, axis=-1, keepdims=True) + NORM_EPS)


def _gelu(x):
    return 0.5 * x * (1.0 + lax.erf(x * (2.0 ** -0.5)))


def _seg_rms(z, seg_ref, gain):
    outs = []
    for j in range(z.shape[1] // MXU_DIM):
        zj = z[:, j * MXU_DIM:(j + 1) * MXU_DIM]
        ms = _dot((zj * zj).astype(BF16), seg_ref[...])
        outs.append(zj * lax.rsqrt(ms + NORM_EPS))
    return jnp.concatenate(outs, axis=1) * gain


def _row_tile(t):
    return 512 if t % 512 == 0 else t


def _even_in_kernel(x_ref, g_ref, wq, wk, wv, wf, wu, wvg, bq, bk, bv, bf, bu, bvg, qg, kg, lng, lnb, seg,
                    q_o, k_o, kb_o, v_o, vb_o, lf_o, u_o, vn_o):
    xn = (_rms(x_ref[...]) * g_ref[...]).astype(BF16)

    def proj(w, b):
        return _dot(xn, w[...]) + b[...]

    q = _seg_rms(proj(wq, bq), seg, qg[...])
    q_o[...] = (q * FOX_SCALE).astype(BF16)
    k = _seg_rms(proj(wk, bk), seg, kg[...])
    k_o[...] = k
    kb_o[...] = k.astype(BF16)
    v = proj(wv, bv)
    v_o[...] = v
    vb_o[...] = v.astype(BF16)
    f = proj(wf, bf)
    lf_o[...] = jax.nn.log_sigmoid(f)[:, :FOX_HEADS]
    u_o[...] = _gelu(proj(wu, bu)).astype(BF16)
    vg = _gelu(proj(wvg, bvg))
    parts = []
    for gi in range(SGU_GROUPS):
        c = vg[:, gi * SGU_CH:(gi + 1) * SGU_CH]
        xc = c - jnp.mean(c, axis=-1, keepdims=True)
        parts.append(xc * lax.rsqrt(jnp.mean(xc * xc, axis=-1, keepdims=True) + NORM_EPS))
    vn = jnp.concatenate(parts, axis=1) * lng[...] + lnb[...]
    vn_o[...] = vn.astype(vn_o.dtype)


def _even_in(x, p, vn_dtype):
    t = x.shape[0]
    tm = _row_tile(t)
    row = lambda w: pl.BlockSpec((tm, w), lambda i: (i, 0))
    full = lambda a: pl.BlockSpec(a.shape, lambda i: (0,) * a.ndim)
    ws = [p["wq"], p["wk"], p["wv"], p["wf"], p["wu"], p["wvg"], p["bq"], p["bk"], p["bv"], p["bf"], p["bu"],
          p["bvg"], p["qg"], p["kg"], p["lng"], p["lnb"], p["seg"]]
    return pl.pallas_call(
        _even_in_kernel,
        grid=(t // tm,),
        in_specs=[row(D_MODEL), full(p["g_mix"])] + [full(a) for a in ws],
        out_specs=[row(FOX_W), row(FOX_W), row(FOX_W), row(FOX_W), row(FOX_W), row(FOX_HEADS), row(SGU_W),
                   row(SGU_W)],
        out_shape=[jax.ShapeDtypeStruct((t, FOX_W), BF16), jax.ShapeDtypeStruct((t, FOX_W), F32),
                   jax.ShapeDtypeStruct((t, FOX_W), BF16), jax.ShapeDtypeStruct((t, FOX_W), F32),
                   jax.ShapeDtypeStruct((t, FOX_W), BF16), jax.ShapeDtypeStruct((t, FOX_HEADS), F32),
                   jax.ShapeDtypeStruct((t, SGU_W), BF16), jax.ShapeDtypeStruct((t, SGU_W), vn_dtype)],
        compiler_params=_cp(1),
        name="even_in_proj",
    )(x, p["g_mix"], *ws)


def _cumsum_kernel(x_ref, o_ref):
    r = lax.broadcasted_iota(jnp.int32, (LANES, LANES), 0)
    c = lax.broadcasted_iota(jnp.int32, (LANES, LANES), 1)
    upper = jnp.where(r <= c, 1.0, 0.0).astype(BF16)
    carry = jnp.zeros((x_ref.shape[0], 1), F32)
    for j in range(x_ref.shape[1] // LANES):
        blk = x_ref[:, j * LANES:(j + 1) * LANES]
        out = _dot_split3(blk, upper) + carry
        o_ref[:, j * LANES:(j + 1) * LANES] = out
        carry = out[:, LANES - 1:LANES]


def _cumsum_rows(x):
    return pl.pallas_call(
        _cumsum_kernel,
        out_shape=jax.ShapeDtypeStruct(x.shape, F32),
        name="logf_cumsum",
    )(x)


def _pair_attn_kernel(*refs, fox, tq):
    if fox:
        q_ref, k_ref, v_ref, c_ref, o_ref = refs
    else:
        q_ref, k_ref, v_ref, lam_ref, gain_ref, o_ref = refs
    qi = pl.program_id(2)
    q = q_ref[...]
    lane = lax.broadcasted_iota(jnp.int32, (1, LANES), 1)
    zero = jnp.zeros_like(q)
    q_maps = (jnp.where(lane < FOX_HD, q, zero), jnp.where(lane >= FOX_HD, q, zero))
    row = lax.broadcasted_iota(jnp.int32, (tq, tq), 0)
    col = lax.broadcasted_iota(jnp.int32, (tq, tq), 1)

    def step(kj, carry, masked):
        start = pl.multiple_of(kj * tq, tq)
        k_t = k_ref[pl.ds(start, tq), :]
        v_t = v_ref[pl.ds(start, tq), :]
        new = []
        for h in range(2):
            m, l, acc = carry[h]
            s = _dot_nt(q_maps[h], k_t)
            if fox:
                s = s - c_ref[0, 0, h, pl.ds(kj, 1), :]
            if masked:
                s = jnp.where(row >= col, s, -jnp.inf)
            m_new = jnp.maximum(m, jnp.max(s, axis=1, keepdims=True))
            alpha = jnp.exp(m - m_new)
            p = jnp.exp(s - m_new)
            l = alpha * l + jnp.sum(p, axis=1, keepdims=True)
            acc = alpha * acc + _dot(p.astype(BF16), v_t)
            new.append((m_new, l, acc))
        return tuple(new)

    init = tuple((jnp.full((tq, 1), -jnp.inf, F32), jnp.zeros((tq, 1), F32), jnp.zeros((tq, LANES), F32))
                 for _ in range(2))
    carry = lax.fori_loop(0, qi, lambda kj, c: step(kj, c, False), init)
    (_, l_a, acc_a), (_, l_b, acc_b) = step(qi, carry, True)
    o_a = acc_a * (1.0 / l_a)
    o_b = acc_b * (1.0 / l_b)
    if fox:
        o_ref[...] = jnp.where(lane < FOX_HD, o_a, o_b).astype(o_ref.dtype)
    else:
        lam = _diff_lambda(lam_ref)
        o = o_a - lam * o_b
        o = _rms(o) * gain_ref[...] * (1.0 - LAM_INIT)
        o_ref[...] = o.astype(o_ref.dtype)


def _diff_lambda(lam_ref):
    a = jnp.sum(lam_ref[0:1, :] * lam_ref[1:2, :], axis=1, keepdims=True)
    b = jnp.sum(lam_ref[2:3, :] * lam_ref[3:4, :], axis=1, keepdims=True)
    return jnp.exp(a) - jnp.exp(b) + LAM_INIT


def _pair_attn(q, k, v, batch, seq, extra, fox, tq=512):
    width = q.shape[1]
    n_pairs = width // LANES
    nq = seq // tq
    q_spec = pl.BlockSpec((tq, LANES), lambda b, h, i: (b * nq + i, h))
    kv_spec = pl.BlockSpec((seq, LANES), lambda b, h, i: (b, h))
    if fox:
        (c5,) = extra
        extra_specs = [pl.BlockSpec((1, 1, 2, nq, tq), lambda b, h, i: (b, h, 0, 0, 0))]
    else:
        extra_specs = [pl.BlockSpec(a.shape, lambda b, h, i: (0, 0)) for a in extra]
    return pl.pallas_call(
        functools.partial(_pair_attn_kernel, fox=fox, tq=tq),
        grid=(batch, n_pairs, nq),
        in_specs=[q_spec, kv_spec, kv_spec] + extra_specs,
        out_specs=q_spec,
        out_shape=jax.ShapeDtypeStruct(q.shape, BF16),
        compiler_params=_cp(3),
        name="fox_prompt_attn" if fox else "diff_prompt_attn",
    )(q, k, v, *extra)


def _even_out_kernel(x_ref, fo_ref, u_ref, vn_ref, ws_ref, bs_ref, w1, w2, o_ref, *, single):
    tm = x_ref.shape[0]
    u = u_ref[...].astype(F32)
    if single:
        so = u * (vn_ref[...].astype(F32) * ws_ref[...] + bs_ref[...])
    else:
        r = lax.broadcasted_iota(jnp.int32, (CHUNK, CHUNK), 0)
        c = lax.broadcasted_iota(jnp.int32, (CHUNK, CHUNK), 1)
        tril = [jnp.where(r >= c, ws_ref[g], 0.0).astype(BF16) for g in range(SGU_GROUPS)]
        rows = []
        for ch in range(tm // CHUNK):
            rs = slice(ch * CHUNK, (ch + 1) * CHUNK)
            cols = []
            for g in range(SGU_GROUPS):
                cs = slice(g * SGU_CH, (g + 1) * SGU_CH)
                mixed = _dot(tril[g], vn_ref[rs, cs]) + bs_ref[g]
                cols.append(u[rs, cs] * mixed)
            rows.append(jnp.concatenate(cols, axis=1))
        so = jnp.concatenate(rows, axis=0)
    o_ref[...] = x_ref[...] + _dot(fo_ref[...], w1[...]) + _dot(so.astype(BF16), w2[...])


def _even_out(x, fo, u, vn, ws, bs, w1, w2, single):
    t = x.shape[0]
    tm = _row_tile(t)
    row = lambda w: pl.BlockSpec((tm, w), lambda i: (i, 0))
    full = lambda a: pl.BlockSpec(a.shape, lambda i: (0,) * a.ndim)
    return pl.pallas_call(
        functools.partial(_even_out_kernel, single=single),
        grid=(t // tm,),
        in_specs=[row(D_MODEL), row(FOX_W), row(SGU_W), row(SGU_W), full(ws), full(bs), full(w1), full(w2)],
        out_specs=row(D_MODEL),
        out_shape=jax.ShapeDtypeStruct((t, D_MODEL), F32),
        compiler_params=_cp(1),
        name="even_out_proj",
    )(x, fo, u, vn, ws, bs, w1, w2)


def _swiglu_kernel(x_ref, g_ref, wg, wu, wd, o_ref, xn_s, acc_s):
    f = pl.program_id(1)

    @pl.when(f == 0)
    def _():
        xn_s[...] = (_rms(x_ref[...]) * g_ref[...]).astype(BF16)
        acc_s[...] = jnp.zeros_like(acc_s)

    xn = xn_s[...]
    a = _dot(xn, wg[...])
    b = _dot(xn, wu[...])
    h = (a * jax.nn.sigmoid(a) * b).astype(BF16)
    acc_s[...] += _dot(h, wd[...])

    @pl.when(f == pl.num_programs(1) - 1)
    def _():
        o_ref[...] = x_ref[...] + acc_s[...]


def _swiglu(x, g, wg, wu, wd, tf):
    t = x.shape[0]
    tm = _row_tile(t)
    d_ff = wg.shape[1]
    return pl.pallas_call(
        _swiglu_kernel,
        grid=(t // tm, d_ff // tf),
        in_specs=[pl.BlockSpec((tm, D_MODEL), lambda i, f: (i, 0)),
                  pl.BlockSpec((1, D_MODEL), lambda i, f: (0, 0)),
                  pl.BlockSpec((D_MODEL, tf), lambda i, f: (0, f)),
                  pl.BlockSpec((D_MODEL, tf), lambda i, f: (0, f)),
                  pl.BlockSpec((tf, D_MODEL), lambda i, f: (f, 0))],
        out_specs=pl.BlockSpec((tm, D_MODEL), lambda i, f: (i, 0)),
        out_shape=jax.ShapeDtypeStruct((t, D_MODEL), F32),
        scratch_shapes=[pltpu.VMEM((tm, D_MODEL), BF16), pltpu.VMEM((tm, D_MODEL), F32)],
        compiler_params=_cp(2),
        name="dense_swiglu",
    )(x, g, wg, wu, wd)


def _odd_in_kernel(x_ref, g_ref, wq, wk, wv, qg, kg, seg, q_o, k_o, kb_o, v_o, vb_o):
    xn = (_rms(x_ref[...]) * g_ref[...]).astype(BF16)
    q = _seg_rms(_dot(xn, wq[...]), seg, qg[...])
    q_o[...] = (q * DIFF_SCALE).astype(BF16)
    k = _seg_rms(_dot(xn, wk[...]), seg, kg[...])
    k_o[...] = k
    kb_o[...] = k.astype(BF16)
    v = _dot(xn, wv[...])
    v_o[...] = v
    vb_o[...] = v.astype(BF16)


def _odd_in(x, p):
    t = x.shape[0]
    tm = _row_tile(t)
    row = pl.BlockSpec((tm, DIFF_W), lambda i: (i, 0))
    full = lambda a: pl.BlockSpec(a.shape, lambda i: (0,) * a.ndim)
    ws = [p["g_mix"], p["wq"], p["wk"], p["wv"], p["qg"], p["kg"], p["seg"]]
    return pl.pallas_call(
        _odd_in_kernel,
        grid=(t // tm,),
        in_specs=[row] + [full(a) for a in ws],
        out_specs=[row] * 5,
        out_shape=[jax.ShapeDtypeStruct((t, DIFF_W), BF16), jax.ShapeDtypeStruct((t, DIFF_W), F32),
                   jax.ShapeDtypeStruct((t, DIFF_W), BF16), jax.ShapeDtypeStruct((t, DIFF_W), F32),
                   jax.ShapeDtypeStruct((t, DIFF_W), BF16)],
        compiler_params=_cp(1),
        name="odd_in_proj",
    )(x, *ws)


def _res_proj_kernel(x_ref, a_ref, w_ref, o_ref):
    o_ref[...] = x_ref[...] + _dot(a_ref[...], w_ref[...])


def _res_proj(x, a, w):
    t = x.shape[0]
    tm = _row_tile(t)
    return pl.pallas_call(
        _res_proj_kernel,
        grid=(t // tm,),
        in_specs=[pl.BlockSpec((tm, D_MODEL), lambda i: (i, 0)),
                  pl.BlockSpec((tm, a.shape[1]), lambda i: (i, 0)),
                  pl.BlockSpec(w.shape, lambda i: (0, 0))],
        out_specs=pl.BlockSpec((tm, D_MODEL), lambda i: (i, 0)),
        out_shape=jax.ShapeDtypeStruct((t, D_MODEL), F32),
        compiler_params=_cp(1),
        name="odd_out_proj",
    )(x, a, w)


def _router_kernel(x_ref, g_ref, w_ref, b_ref, xn_o, comb_o):
    xn = _rms(x_ref[...]) * g_ref[...]
    xn_o[...] = xn.astype(BF16)
    w = w_ref[...]
    w_hi = w.astype(BF16)
    w_lo = (w - w_hi.astype(F32)).astype(BF16)
    x_hi = xn.astype(BF16)
    x_lo = (xn - x_hi.astype(F32)).astype(BF16)
    logits = _dot(x_hi, w_hi) + _dot(x_hi, w_lo) + _dot(x_lo, w_hi) + b_ref[...]
    lane = lax.broadcasted_iota(jnp.int32, logits.shape, 1)
    logits = jnp.where(lane < N_EXPERTS, logits, -jnp.inf)
    v1 = jnp.max(logits, axis=1, keepdims=True)
    i1 = jnp.min(jnp.where(logits == v1, lane, LANES), axis=1, keepdims=True)
    rest = jnp.where(lane == i1, -jnp.inf, logits)
    v2 = jnp.max(rest, axis=1, keepdims=True)
    i2 = jnp.min(jnp.where(rest == v2, lane, LANES), axis=1, keepdims=True)
    e2 = jnp.exp(v2 - v1)
    g1 = 1.0 / (1.0 + e2)
    g2 = e2 * g1
    comb_o[...] = jnp.where(lane == i1, g1, 0.0) + jnp.where(lane == i2, g2, 0.0)


def _router(x, g, w, b):
    t = x.shape[0]
    tm = _row_tile(t)
    return pl.pallas_call(
        _router_kernel,
        grid=(t // tm,),
        in_specs=[pl.BlockSpec((tm, D_MODEL), lambda i: (i, 0)),
                  pl.BlockSpec((1, D_MODEL), lambda i: (0, 0)),
                  pl.BlockSpec((D_MODEL, LANES), lambda i: (0, 0)),
                  pl.BlockSpec((1, LANES), lambda i: (0, 0))],
        out_specs=[pl.BlockSpec((tm, D_MODEL), lambda i: (i, 0)), pl.BlockSpec((tm, LANES), lambda i: (i, 0))],
        out_shape=[jax.ShapeDtypeStruct((t, D_MODEL), BF16), jax.ShapeDtypeStruct((t, LANES), F32)],
        compiler_params=_cp(1),
        name="moe_router",
    )(x, g, w, b)


def _moe_kernel(y_ref, xn_ref, comb_ref, wg, wu, wd, o_ref, acc_s):
    e = pl.program_id(1)
    f = pl.program_id(2)

    @pl.when((e == 0) & (f == 0))
    def _():
        acc_s[...] = jnp.zeros_like(acc_s)

    xn = xn_ref[...]
    a = _dot(xn, wg[0])
    b = _dot(xn, wu[0])
    h = (a * jax.nn.sigmoid(a) * b).astype(BF16)
    comb = comb_ref[...]
    lane = lax.broadcasted_iota(jnp.int32, comb.shape, 1)
    gate = jnp.sum(jnp.where(lane == e, comb, 0.0), axis=1, keepdims=True)
    acc_s[...] += gate * _dot(h, wd[0])

    @pl.when((e == pl.num_programs(1) - 1) & (f == pl.num_programs(2) - 1))
    def _():
        o_ref[...] = y_ref[...] + acc_s[...]


def _moe(y, xn, comb, wg, wu, wd, tf):
    t = y.shape[0]
    tm = _row_tile(t)
    n_e, _, d_ff = wg.shape
    return pl.pallas_call(
        _moe_kernel,
        grid=(t // tm, n_e, d_ff // tf),
        in_specs=[pl.BlockSpec((tm, D_MODEL), lambda i, e, f: (i, 0)),
                  pl.BlockSpec((tm, D_MODEL), lambda i, e, f: (i, 0)),
                  pl.BlockSpec((tm, LANES), lambda i, e, f: (i, 0)),
                  pl.BlockSpec((1, D_MODEL, tf), lambda i, e, f: (e, 0, f)),
                  pl.BlockSpec((1, D_MODEL, tf), lambda i, e, f: (e, 0, f)),
                  pl.BlockSpec((1, tf, D_MODEL), lambda i, e, f: (e, f, 0))],
        out_specs=pl.BlockSpec((tm, D_MODEL), lambda i, e, f: (i, 0)),
        out_shape=jax.ShapeDtypeStruct((t, D_MODEL), F32),
        scratch_shapes=[pltpu.VMEM((tm, D_MODEL), F32)],
        compiler_params=_cp(3),
        name="moe_experts",
    )(y, xn, comb, wg, wu, wd)


def _fox_sample_kernel(pt_ref, q_ref, kn_ref, vn_ref, lfn_ref, *refs, n_pages):
    k_refs = refs[:n_pages]
    v_refs = refs[n_pages:2 * n_pages]
    lf_refs = refs[2 * n_pages:3 * n_pages]
    o_ref = refs[3 * n_pages]
    q = q_ref[0]
    sub = lax.broadcasted_iota(jnp.int32, (FOX_HEADS, FOX_W), 0)
    lane = lax.broadcasted_iota(jnp.int32, (FOX_HEADS, FOX_W), 1)
    sel = (lane >> 6) == sub
    qbd = jnp.where(sel, q.astype(F32), 0.0).astype(BF16)
    r = lax.broadcasted_iota(jnp.int32, (PAGE_SIZE, PAGE_SIZE), 0)
    c = lax.broadcasted_iota(jnp.int32, (PAGE_SIZE, PAGE_SIZE), 1)
    later = jnp.where(r > c, 1.0, 0.0).astype(BF16)
    lf_new = lfn_ref[0]
    tail = jnp.zeros((FOX_HEADS, 1), F32)
    scores = [None] * n_pages
    for p in reversed(range(n_pages)):
        lf = lf_refs[p][0]
        bias = _dot_split3(lf, later) + tail + lf_new
        tail = tail + jnp.sum(lf, axis=1, keepdims=True)
        scores[p] = _dot_nt(qbd, k_refs[p][0].astype(BF16)) + bias
    s_new = jnp.sum(jnp.where(sel, q.astype(F32) * kn_ref[0], 0.0), axis=1, keepdims=True)
    m = s_new
    for s in scores:
        m = jnp.maximum(m, jnp.max(s, axis=1, keepdims=True))
    p_new = jnp.exp(s_new - m)
    l = p_new
    acc = p_new * vn_ref[0]
    for p in range(n_pages):
        pr = jnp.exp(scores[p] - m)
        l = l + jnp.sum(pr, axis=1, keepdims=True)
        acc = acc + _dot(pr.astype(BF16), v_refs[p][0].astype(BF16))
    acc = acc * (1.0 / l)
    o_ref[0] = jnp.sum(jnp.where(sel, acc, 0.0), axis=0, keepdims=True).astype(o_ref.dtype)


def _fox_sample(page_table, q, k_new, v_new, lf_new, k_pool, v_pool, lf_pool_t):
    n_seq, n_pages = page_table.shape
    vec = lambda w: pl.BlockSpec((1, 1, w), lambda b, pt: (b, 0, 0))
    page = lambda shape, p: pl.BlockSpec((1,) + shape, lambda b, pt, p=p: (pt[b, p], 0, 0))
    in_specs = [vec(FOX_W), vec(FOX_W), vec(FOX_W), pl.BlockSpec((1, FOX_HEADS, 1), lambda b, pt: (b, 0, 0))]
    in_specs += [page((PAGE_SIZE, FOX_W), p) for p in range(n_pages)]
    in_specs += [page((PAGE_SIZE, FOX_W), p) for p in range(n_pages)]
    in_specs += [page((FOX_HEADS, PAGE_SIZE), p) for p in range(n_pages)]
    return pl.pallas_call(
        functools.partial(_fox_sample_kernel, n_pages=n_pages),
        grid_spec=pltpu.PrefetchScalarGridSpec(
            num_scalar_prefetch=1, grid=(n_seq,), in_specs=in_specs, out_specs=vec(FOX_W)),
        out_shape=jax.ShapeDtypeStruct((n_seq, 1, FOX_W), BF16),
        compiler_params=_cp(1),
        name="fox_sample_attn",
    )(page_table, q, k_new, v_new, lf_new, *([k_pool] * n_pages), *([v_pool] * n_pages),
      *([lf_pool_t] * n_pages))


def _diff_sample_kernel(pt_ref, q_ref, kn_ref, vn_ref, lam_ref, gain_ref, *refs, n_pages):
    k_refs = refs[:n_pages]
    v_refs = refs[n_pages:2 * n_pages]
    o_ref = refs[2 * n_pages]
    n_maps = 2 * DIFF_HEADS
    q = q_ref[0]
    sub = lax.broadcasted_iota(jnp.int32, (n_maps, DIFF_W), 0)
    lane = lax.broadcasted_iota(jnp.int32, (n_maps, DIFF_W), 1)
    sel = (lane >> 6) == ((sub & (DIFF_HEADS - 1)) * 2 + (sub >> 3))
    qbd = jnp.where(sel, q.astype(F32), 0.0).astype(BF16)
    scores = [_dot_nt(qbd, k_refs[p][0].astype(BF16)) for p in range(n_pages)]
    s_new = jnp.sum(jnp.where(sel, q.astype(F32) * kn_ref[0], 0.0), axis=1, keepdims=True)
    m = s_new
    for s in scores:
        m = jnp.maximum(m, jnp.max(s, axis=1, keepdims=True))
    probs = [jnp.exp(s - m) for s in scores]
    p_new = jnp.exp(s_new - m)
    l = p_new
    for pr in probs:
        l = l + jnp.sum(pr, axis=1, keepdims=True)
    inv_l = 1.0 / l
    lam = _diff_lambda(lam_ref)

    def mix(pr):
        pn = pr * inv_l
        return pn[:DIFF_HEADS] - lam * pn[DIFF_HEADS:]

    acc = mix(p_new) * vn_ref[0]
    for p in range(n_pages):
        acc = acc + _dot(mix(probs[p]).astype(BF16), v_refs[p][0].astype(BF16))
    sub8 = lax.broadcasted_iota(jnp.int32, (DIFF_HEADS, DIFF_W), 0)
    lane8 = lax.broadcasted_iota(jnp.int32, (DIFF_HEADS, DIFF_W), 1)
    own = (lane8 >> 7) == sub8
    acc = jnp.where(own, acc, 0.0)
    ms = jnp.sum(acc * acc, axis=1, keepdims=True) * (1.0 / (2 * DIFF_HD))
    acc = acc * lax.rsqrt(ms + NORM_EPS)
    o = jnp.sum(acc, axis=0, keepdims=True) * gain_ref[...] * (1.0 - LAM_INIT)
    o_ref[0] = o.astype(o_ref.dtype)


def _diff_sample(page_table, q, k_new, v_new, lam_vecs, gain_t, k_pool, v_pool):
    n_seq, n_pages = page_table.shape
    vec = lambda w: pl.BlockSpec((1, 1, w), lambda b, pt: (b, 0, 0))
    page = lambda p: pl.BlockSpec((1, PAGE_SIZE, DIFF_W), lambda b, pt, p=p: (pt[b, p], 0, 0))
    in_specs = [vec(DIFF_W), vec(DIFF_W), vec(DIFF_W),
                pl.BlockSpec(lam_vecs.shape, lambda b, pt: (0, 0)),
                pl.BlockSpec(gain_t.shape, lambda b, pt: (0, 0))]
    in_specs += [page(p) for p in range(n_pages)] * 2
    return pl.pallas_call(
        functools.partial(_diff_sample_kernel, n_pages=n_pages),
        grid_spec=pltpu.PrefetchScalarGridSpec(
            num_scalar_prefetch=1, grid=(n_seq,), in_specs=in_specs, out_specs=vec(DIFF_W)),
        out_shape=jax.ShapeDtypeStruct((n_seq, 1, DIFF_W), BF16),
        compiler_params=_cp(1, 60 * 1024 * 1024),
        name="diff_sample_attn",
    )(page_table, q, k_new, v_new, lam_vecs, gain_t, *([k_pool] * n_pages), *([v_pool] * n_pages))


def _seg_matrix():
    i = jnp.arange(MXU_DIM)
    return jnp.where((i[:, None] // FOX_HD) == (i[None, :] // FOX_HD), 1.0 / FOX_HD, 0.0).astype(BF16)


def kernel(x_prompt, x_sample, cache_fox_k, cache_fox_v, cache_fox_logf, cache_diff_k, cache_diff_v, page_table,
           even_norm_mix, even_w_in, even_b_in, fox_q_gain, fox_k_gain, sgu_ln_gain, sgu_ln_bias, sgu_w_s, sgu_b_s,
           even_w_out, even_norm_ffn, ffn_w_gate, ffn_w_up, ffn_w_down,
           odd_norm_mix, odd_w_in, diff_q_gain, diff_k_gain, diff_lambda_q1, diff_lambda_k1, diff_lambda_q2,
           diff_lambda_k2, diff_subln_gain, odd_w_out, odd_norm_ffn, router_w, router_b,
           moe_w_gate, moe_w_up, moe_w_down):
    batch, seq, _ = x_prompt.shape
    n_seq = x_sample.shape[0]
    n_phys = cache_fox_k.shape[1]
    seg = _seg_matrix()
    row = lambda a: a.reshape(1, -1).astype(F32)

    w_in, b_in = even_w_in[0], even_b_in[0]
    o_f = 3 * FOX_W
    o_u = o_f + FOX_HEADS
    o_vg = o_u + SGU_W
    pad_f = LANES - FOX_HEADS
    pe = dict(
        g_mix=row(even_norm_mix[0]),
        wq=w_in[:, :FOX_W].astype(BF16), wk=w_in[:, FOX_W:2 * FOX_W].astype(BF16),
        wv=w_in[:, 2 * FOX_W:o_f].astype(BF16),
        wf=jnp.pad(w_in[:, o_f:o_u], ((0, 0), (0, pad_f))).astype(BF16),
        wu=w_in[:, o_u:o_vg].astype(BF16), wvg=w_in[:, o_vg:].astype(BF16),
        bq=row(b_in[:FOX_W]), bk=row(b_in[FOX_W:2 * FOX_W]), bv=row(b_in[2 * FOX_W:o_f]),
        bf=row(jnp.pad(b_in[o_f:o_u], (0, pad_f))), bu=row(b_in[o_u:o_vg]), bvg=row(b_in[o_vg:]),
        qg=row(jnp.tile(fox_q_gain[0], FOX_HEADS)), kg=row(jnp.tile(fox_k_gain[0], FOX_HEADS)),
        lng=row(sgu_ln_gain[0]), lnb=row(sgu_ln_bias[0]), seg=seg)
    w_s = sgu_w_s[0]
    b_s_full = jnp.broadcast_to(sgu_b_s[0][:, :, None], (SGU_GROUPS, CHUNK, SGU_CH)).astype(F32)
    w_s_first = row(jnp.repeat(w_s[:, 0, 0], SGU_CH))
    b_s_first = row(jnp.repeat(sgu_b_s[0][:, 0], SGU_CH))
    w_out1 = even_w_out[0][:FOX_W].astype(BF16)
    w_out2 = even_w_out[0][FOX_W:].astype(BF16)
    g_ffn = row(even_norm_ffn[0])
    ffn_g, ffn_u, ffn_d = ffn_w_gate[0].astype(BF16), ffn_w_up[0].astype(BF16), ffn_w_down[0].astype(BF16)

    wo_in = odd_w_in[0]
    po = dict(
        g_mix=row(odd_norm_mix[0]),
        wq=wo_in[:, :DIFF_W].astype(BF16), wk=wo_in[:, DIFF_W:2 * DIFF_W].astype(BF16),
        wv=wo_in[:, 2 * DIFF_W:].astype(BF16),
        qg=row(jnp.tile(diff_q_gain[0].reshape(-1), DIFF_HEADS)),
        kg=row(jnp.tile(diff_k_gain[0].reshape(-1), DIFF_HEADS)), seg=seg)
    lam_vecs = jnp.stack([diff_lambda_q1[0], diff_lambda_k1[0], diff_lambda_q2[0], diff_lambda_k2[0]]).astype(F32)
    subln = row(diff_subln_gain[0])
    subln_t = row(jnp.tile(diff_subln_gain[0], DIFF_HEADS))
    w_oo = odd_w_out[0].astype(BF16)
    g_moe = row(odd_norm_ffn[0])
    r_w = jnp.pad(router_w[0], ((0, 0), (0, LANES - N_EXPERTS))).astype(F32)
    r_b = row(jnp.pad(router_b[0], (0, LANES - N_EXPERTS)))
    moe_g, moe_u, moe_d = moe_w_gate[0].astype(BF16), moe_w_up[0].astype(BF16), moe_w_down[0].astype(BF16)

    xp = x_prompt.reshape(batch * seq, D_MODEL)
    xs = x_sample.reshape(n_seq, D_MODEL)

    q, fk_p, kb, fv_p, vb, lf_p, u, vn = _even_in(xp, pe, BF16)
    tq = 512
    lf_t = lf_p.reshape(batch, seq, FOX_HEADS).transpose(0, 2, 1).reshape(batch * FOX_HEADS, seq)
    c5 = _cumsum_rows(lf_t).reshape(batch, FOX_HEADS // 2, 2, seq // tq, tq)
    fo = _pair_attn(q, kb, vb, batch, seq, (c5,), fox=True, tq=tq)
    yp = _even_out(xp, fo, u, vn, w_s, b_s_full, w_out1, w_out2, single=False)
    yp = _swiglu(yp, g_ffn, ffn_g, ffn_u, ffn_d, tf=1408)

    qs, fk_s, _, fv_s, _, lf_s, us, vn_s = _even_in(xs, pe, F32)
    fos = _fox_sample(
        page_table, qs.reshape(n_seq, 1, FOX_W), fk_s.reshape(n_seq, 1, FOX_W), fv_s.reshape(n_seq, 1, FOX_W),
        lf_s.reshape(n_seq, FOX_HEADS, 1),
        cache_fox_k[0].reshape(n_phys, PAGE_SIZE, FOX_W), cache_fox_v[0].reshape(n_phys, PAGE_SIZE, FOX_W),
        cache_fox_logf[0].transpose(0, 2, 1))
    ys = _even_out(xs, fos.reshape(n_seq, FOX_W), us, vn_s, w_s_first, b_s_first, w_out1, w_out2, single=True)
    ys = _swiglu(ys, g_ffn, ffn_g, ffn_u, ffn_d, tf=1408)

    q, dk_p, kb, dv_p, vb = _odd_in(yp, po)
    do = _pair_attn(q, kb, vb, batch, seq, (lam_vecs, subln), fox=False, tq=tq)
    yp = _res_proj(yp, do, w_oo)
    xn, comb = _router(yp, g_moe, r_w, r_b)
    yp = _moe(yp, xn, comb, moe_g, moe_u, moe_d, tf=896)

    qs, dk_s, _, dv_s, _ = _odd_in(ys, po)
    dos = _diff_sample(
        page_table, qs.reshape(n_seq, 1, DIFF_W), dk_s.reshape(n_seq, 1, DIFF_W), dv_s.reshape(n_seq, 1, DIFF_W),
        lam_vecs, subln_t,
        cache_diff_k[0].reshape(n_phys, PAGE_SIZE, DIFF_W), cache_diff_v[0].reshape(n_phys, PAGE_SIZE, DIFF_W))
    ys = _res_proj(ys, dos.reshape(n_seq, DIFF_W), w_oo)
    xn, comb = _router(ys, g_moe, r_w, r_b)
    ys = _moe(ys, xn, comb, moe_g, moe_u, moe_d, tf=896)

    dec = x_sample.shape[1]
    return (yp.reshape(batch, seq, D_MODEL), ys.reshape(n_seq, dec, D_MODEL),
            fk_p.reshape(1, batch, seq, FOX_HEADS, FOX_HD), fv_p.reshape(1, batch, seq, FOX_HEADS, FOX_HD),
            lf_p.reshape(1, batch, seq, FOX_HEADS),
            dk_p.reshape(1, batch, seq, DIFF_HEADS, 2, DIFF_HD), dv_p.reshape(1, batch, seq, DIFF_HEADS, 2 * DIFF_HD),
            fk_s.reshape(1, n_seq, dec, FOX_HEADS, FOX_HD), fv_s.reshape(1, n_seq, dec, FOX_HEADS, FOX_HD),
            lf_s.reshape(1, n_seq, dec, FOX_HEADS),
            dk_s.reshape(1, n_seq, dec, DIFF_HEADS, 2, DIFF_HD), dv_s.reshape(1, n_seq, dec, DIFF_HEADS, 2 * DIFF_HD),
            vn_s.reshape(1, n_seq, dec, SGU_GROUPS, SGU_CH))
```

```python
import functools
import math

import jax
import jax.numpy as jnp
from jax import lax
from jax.experimental import pallas as pl
from jax.experimental.pallas import tpu as pltpu

F32 = jnp.float32
BF16 = jnp.bfloat16

D_MODEL = 1024
FOX_HEADS = 8
FOX_HD = 64
FOX_W = 512
SGU_GROUPS = 4
SGU_CH = 128
SGU_W = 512
CHUNK = 128
DIFF_HEADS = 8
DIFF_HD = 64
DIFF_W = 1024
N_EXPERTS = 8
PAGE_SIZE = 128
NORM_EPS = 1e-6
FOX_SCALE = FOX_HD ** -0.5
DIFF_SCALE = DIFF_HD ** -0.5
LAM_INIT = 0.8 - 0.6 * math.exp(-0.3 * 1)

LANES = 128
MXU_DIM = 256
VMEM_LIMIT = 52 * 1024 * 1024


def _cp(n_axes, vmem=VMEM_LIMIT):
    return pltpu.CompilerParams(dimension_semantics=("arbitrary",) * n_axes, vmem_limit_bytes=vmem)


def _dot(a, b):
    return jnp.dot(a, b, preferred_element_type=F32)


def _dot_nt(a, b):
    return lax.dot_general(a, b, (((1,), (1,)), ((), ())), preferred_element_type=F32)


def _dot_split3(x, m_bf16):
    hi = x.astype(BF16)
    r1 = x - hi.astype(F32)
    mid = r1.astype(BF16)
    lo = (r1 - mid.astype(F32)).astype(BF16)
    return _dot(hi, m_bf16) + _dot(mid, m_bf16) + _dot(lo, m_bf16)


def _rms(x):
    return x * lax.rsqrt(jnp.mean(x * x, axis=-1, keepdims=True) + NORM_EPS)


def _gelu(x):
    return 0.5 * x * (1.0 + lax.erf(x * (2.0 ** -0.5)))


def _seg_rms(z, seg_ref, gain):
    outs = []
    for j in range(z.shape[1] // MXU_DIM):
        zj = z[:, j * MXU_DIM:(j + 1) * MXU_DIM]
        ms = _dot((zj * zj).astype(BF16), seg_ref[...])
        outs.append(zj * lax.rsqrt(ms + NORM_EPS))
    return jnp.concatenate(outs, axis=1) * gain


def _row_tile(t):
    return 512 if t % 512 == 0 else t


def _even_in_kernel(x_ref, g_ref, wq, wk, wv, wf, wu, wvg, bq, bk, bv, bf, bu, bvg, qg, kg, lng, lnb, seg,
                    q_o, kt_o, vt_o, lft_o, u_o, vn_o, *extra, sample):
    xn = (_rms(x_ref[...]) * g_ref[...]).astype(BF16)

    def proj(w, b):
        return _dot(xn, w[...]) + b[...]

    q = _seg_rms(proj(wq, bq), seg, qg[...])
    q_o[...] = (q * FOX_SCALE).astype(BF16)
    k = _seg_rms(proj(wk, bk), seg, kg[...])
    kt_o[0] = k.T
    v = proj(wv, bv)
    vt_o[0] = v.T
    lf = jax.nn.log_sigmoid(proj(wf, bf))
    lft_o[0] = lf.T[:FOX_HEADS]
    if sample:
        k_o, v_o, lf_o = extra
        k_o[...] = k
        v_o[...] = v
        lf_o[...] = lf[:, :FOX_HEADS]
    else:
        kb_o, vb_o = extra
        kb_o[...] = k.astype(BF16)
        vb_o[...] = v.astype(BF16)
    u_o[...] = _gelu(proj(wu, bu)).astype(BF16)
    vg = _gelu(proj(wvg, bvg))
    parts = []
    for gi in range(SGU_GROUPS):
        c = vg[:, gi * SGU_CH:(gi + 1) * SGU_CH]
        xc = c - jnp.mean(c, axis=-1, keepdims=True)
        parts.append(xc * lax.rsqrt(jnp.mean(xc * xc, axis=-1, keepdims=True) + NORM_EPS))
    vn = jnp.concatenate(parts, axis=1) * lng[...] + lnb[...]
    vn_o[...] = vn.astype(vn_o.dtype)


def _even_in(x, p, batch, seq, sample):
    t = x.shape[0]
    tm = _row_tile(seq)
    n_s = seq // tm
    row = lambda w: pl.BlockSpec((tm, w), lambda i: (i, 0))
    col = lambda w: pl.BlockSpec((1, w, tm), lambda i: (i // n_s, 0, i % n_s))
    full = lambda a: pl.BlockSpec(a.shape, lambda i: (0,) * a.ndim)
    ws = [p["wq"], p["wk"], p["wv"], p["wf"], p["wu"], p["wvg"], p["bq"], p["bk"], p["bv"], p["bf"], p["bu"],
          p["bvg"], p["qg"], p["kg"], p["lng"], p["lnb"], p["seg"]]
    nat = lambda w, dt: jax.ShapeDtypeStruct((t, w), dt)
    tr = lambda w: jax.ShapeDtypeStruct((batch, w, seq), F32)
    out_specs = [row(FOX_W), col(FOX_W), col(FOX_W), col(FOX_HEADS), row(SGU_W), row(SGU_W)]
    out_shape = [nat(FOX_W, BF16), tr(FOX_W), tr(FOX_W), tr(FOX_HEADS), nat(SGU_W, BF16),
                 nat(SGU_W, F32 if sample else BF16)]
    if sample:
        out_specs += [row(FOX_W), row(FOX_W), row(FOX_HEADS)]
        out_shape += [nat(FOX_W, F32), nat(FOX_W, F32), nat(FOX_HEADS, F32)]
    else:
        out_specs += [row(FOX_W), row(FOX_W)]
        out_shape += [nat(FOX_W, BF16), nat(FOX_W, BF16)]
    return pl.pallas_call(
        functools.partial(_even_in_kernel, sample=sample),
        grid=(t // tm,),
        in_specs=[row(D_MODEL), full(p["g_mix"])] + [full(a) for a in ws],
        out_specs=out_specs,
        out_shape=out_shape,
        compiler_params=_cp(1),
        name="even_in_proj",
    )(x, p["g_mix"], *ws)


def _cumsum_kernel(x_ref, o_ref):
    r = lax.broadcasted_iota(jnp.int32, (LANES, LANES), 0)
    c = lax.broadcasted_iota(jnp.int32, (LANES, LANES), 1)
    upper = jnp.where(r <= c, 1.0, 0.0).astype(BF16)
    carry = jnp.zeros((x_ref.shape[0], 1), F32)
    for j in range(x_ref.shape[1] // LANES):
        blk = x_ref[:, j * LANES:(j + 1) * LANES]
        out = _dot_split3(blk, upper) + carry
        o_ref[:, j * LANES:(j + 1) * LANES] = out
        carry = out[:, LANES - 1:LANES]


def _cumsum_rows(x):
    return pl.pallas_call(
        _cumsum_kernel,
        out_shape=jax.ShapeDtypeStruct(x.shape, F32),
        name="logf_cumsum",
    )(x)


def _pair_attn_kernel(*refs, fox, tq):
    if fox:
        q_ref, k_ref, v_ref, c_ref, o_ref = refs
    else:
        q_ref, k_ref, v_ref, lam_ref, gain_ref, o_ref = refs
    qi = pl.program_id(2)
    q = q_ref[...]
    lane = lax.broadcasted_iota(jnp.int32, (1, LANES), 1)
    zero = jnp.zeros_like(q)
    q_maps = (jnp.where(lane < FOX_HD, q, zero), jnp.where(lane >= FOX_HD, q, zero))
    row = lax.broadcasted_iota(jnp.int32, (tq, tq), 0)
    col = lax.broadcasted_iota(jnp.int32, (tq, tq), 1)

    def step(kj, carry, masked):
        start = pl.multiple_of(kj * tq, tq)
        k_t = k_ref[pl.ds(start, tq), :]
        v_t = v_ref[pl.ds(start, tq), :]
        new = []
        for h in range(2):
            m, l, acc = carry[h]
            s = _dot_nt(q_maps[h], k_t)
            if fox:
                s = s - c_ref[0, 0, h, pl.ds(kj, 1), :]
            if masked:
                s = jnp.where(row >= col, s, -jnp.inf)
            m_new = jnp.maximum(m, jnp.max(s, axis=1, keepdims=True))
            alpha = jnp.exp(m - m_new)
            p = jnp.exp(s - m_new)
            l = alpha * l + jnp.sum(p, axis=1, keepdims=True)
            acc = alpha * acc + _dot(p.astype(BF16), v_t)
            new.append((m_new, l, acc))
        return tuple(new)

    init = tuple((jnp.full((tq, 1), -jnp.inf, F32), jnp.zeros((tq, 1), F32), jnp.zeros((tq, LANES), F32))
                 for _ in range(2))
    carry = lax.fori_loop(0, qi, lambda kj, c: step(kj, c, False), init)
    (_, l_a, acc_a), (_, l_b, acc_b) = step(qi, carry, True)
    o_a = acc_a * (1.0 / l_a)
    o_b = acc_b * (1.0 / l_b)
    if fox:
        o_ref[...] = jnp.where(lane < FOX_HD, o_a, o_b).astype(o_ref.dtype)
    else:
        lam = _diff_lambda(lam_ref)
        o = o_a - lam * o_b
        o = _rms(o) * gain_ref[...] * (1.0 - LAM_INIT)
        o_ref[...] = o.astype(o_ref.dtype)


def _diff_lambda(lam_ref):
    a = jnp.sum(lam_ref[0:1, :] * lam_ref[1:2, :], axis=1, keepdims=True)
    b = jnp.sum(lam_ref[2:3, :] * lam_ref[3:4, :], axis=1, keepdims=True)
    return jnp.exp(a) - jnp.exp(b) + LAM_INIT


def _pair_attn(q, k, v, batch, seq, extra, fox, tq=512):
    width = q.shape[1]
    n_pairs = width // LANES
    nq = seq // tq
    q_spec = pl.BlockSpec((tq, LANES), lambda b, h, i: (b * nq + i, h))
    kv_spec = pl.BlockSpec((seq, LANES), lambda b, h, i: (b, h))
    if fox:
        (c5,) = extra
        extra_specs = [pl.BlockSpec((1, 1, 2, nq, tq), lambda b, h, i: (b, h, 0, 0, 0))]
    else:
        extra_specs = [pl.BlockSpec(a.shape, lambda b, h, i: (0, 0)) for a in extra]
    return pl.pallas_call(
        functools.partial(_pair_attn_kernel, fox=fox, tq=tq),
        grid=(batch, n_pairs, nq),
        in_specs=[q_spec, kv_spec, kv_spec] + extra_specs,
        out_specs=q_spec,
        out_shape=jax.ShapeDtypeStruct(q.shape, BF16),
        compiler_params=_cp(3),
        name="fox_prompt_attn" if fox else "diff_prompt_attn",
    )(q, k, v, *extra)


def _even_out_kernel(x_ref, fo_ref, u_ref, vn_ref, ws_ref, bs_ref, w1, w2, o_ref, *, single):
    tm = x_ref.shape[0]
    u = u_ref[...].astype(F32)
    if single:
        so = u * (vn_ref[...].astype(F32) * ws_ref[...] + bs_ref[...])
    else:
        r = lax.broadcasted_iota(jnp.int32, (CHUNK, CHUNK), 0)
        c = lax.broadcasted_iota(jnp.int32, (CHUNK, CHUNK), 1)
        tril = [jnp.where(r >= c, ws_ref[g], 0.0).astype(BF16) for g in range(SGU_GROUPS)]
        rows = []
        for ch in range(tm // CHUNK):
            rs = slice(ch * CHUNK, (ch + 1) * CHUNK)
            cols = []
            for g in range(SGU_GROUPS):
                cs = slice(g * SGU_CH, (g + 1) * SGU_CH)
                mixed = _dot(tril[g], vn_ref[rs, cs]) + bs_ref[g]
                cols.append(u[rs, cs] * mixed)
            rows.append(jnp.concatenate(cols, axis=1))
        so = jnp.concatenate(rows, axis=0)
    o_ref[...] = x_ref[...] + _dot(fo_ref[...], w1[...]) + _dot(so.astype(BF16), w2[...])


def _even_out(x, fo, u, vn, ws, bs, w1, w2, single):
    t = x.shape[0]
    tm = _row_tile(t)
    row = lambda w: pl.BlockSpec((tm, w), lambda i: (i, 0))
    full = lambda a: pl.BlockSpec(a.shape, lambda i: (0,) * a.ndim)
    return pl.pallas_call(
        functools.partial(_even_out_kernel, single=single),
        grid=(t // tm,),
        in_specs=[row(D_MODEL), row(FOX_W), row(SGU_W), row(SGU_W), full(ws), full(bs), full(w1), full(w2)],
        out_specs=row(D_MODEL),
        out_shape=jax.ShapeDtypeStruct((t, D_MODEL), F32),
        compiler_params=_cp(1),
        name="even_out_proj",
    )(x, fo, u, vn, ws, bs, w1, w2)


def _swiglu_kernel(x_ref, g_ref, wg, wu, wd, o_ref, xn_s, acc_s):
    f = pl.program_id(1)

    @pl.when(f == 0)
    def _():
        xn_s[...] = (_rms(x_ref[...]) * g_ref[...]).astype(BF16)
        acc_s[...] = jnp.zeros_like(acc_s)

    xn = xn_s[...]
    a = _dot(xn, wg[...])
    b = _dot(xn, wu[...])
    h = (a * jax.nn.sigmoid(a) * b).astype(BF16)
    acc_s[...] += _dot(h, wd[...])

    @pl.when(f == pl.num_programs(1) - 1)
    def _():
        o_ref[...] = x_ref[...] + acc_s[...]


def _swiglu(x, g, wg, wu, wd, tf):
    t = x.shape[0]
    tm = _row_tile(t)
    d_ff = wg.shape[1]
    return pl.pallas_call(
        _swiglu_kernel,
        grid=(t // tm, d_ff // tf),
        in_specs=[pl.BlockSpec((tm, D_MODEL), lambda i, f: (i, 0)),
                  pl.BlockSpec((1, D_MODEL), lambda i, f: (0, 0)),
                  pl.BlockSpec((D_MODEL, tf), lambda i, f: (0, f)),
                  pl.BlockSpec((D_MODEL, tf), lambda i, f: (0, f)),
                  pl.BlockSpec((tf, D_MODEL), lambda i, f: (f, 0))],
        out_specs=pl.BlockSpec((tm, D_MODEL), lambda i, f: (i, 0)),
        out_shape=jax.ShapeDtypeStruct((t, D_MODEL), F32),
        scratch_shapes=[pltpu.VMEM((tm, D_MODEL), BF16), pltpu.VMEM((tm, D_MODEL), F32)],
        compiler_params=_cp(2),
        name="dense_swiglu",
    )(x, g, wg, wu, wd)


def _odd_in_kernel(x_ref, g_ref, wq, wk, wv, qg, kg, seg, q_o, kt_o, v_o, *extra, sample):
    xn = (_rms(x_ref[...]) * g_ref[...]).astype(BF16)
    q = _seg_rms(_dot(xn, wq[...]), seg, qg[...])
    q_o[...] = (q * DIFF_SCALE).astype(BF16)
    k = _seg_rms(_dot(xn, wk[...]), seg, kg[...])
    kt_o[0] = k.T
    v = _dot(xn, wv[...])
    v_o[...] = v
    if sample:
        (k_o,) = extra
        k_o[...] = k
    else:
        kb_o, vb_o = extra
        kb_o[...] = k.astype(BF16)
        vb_o[...] = v.astype(BF16)


def _odd_in(x, p, batch, seq, sample):
    t = x.shape[0]
    tm = _row_tile(seq)
    n_s = seq // tm
    row = pl.BlockSpec((tm, DIFF_W), lambda i: (i, 0))
    col = pl.BlockSpec((1, DIFF_W, tm), lambda i: (i // n_s, 0, i % n_s))
    full = lambda a: pl.BlockSpec(a.shape, lambda i: (0,) * a.ndim)
    ws = [p["g_mix"], p["wq"], p["wk"], p["wv"], p["qg"], p["kg"], p["seg"]]
    nat = lambda dt: jax.ShapeDtypeStruct((t, DIFF_W), dt)
    out_specs = [row, col, row]
    out_shape = [nat(BF16), jax.ShapeDtypeStruct((batch, DIFF_W, seq), F32), nat(F32)]
    if sample:
        out_specs += [row]
        out_shape += [nat(F32)]
    else:
        out_specs += [row, row]
        out_shape += [nat(BF16), nat(BF16)]
    return pl.pallas_call(
        functools.partial(_odd_in_kernel, sample=sample),
        grid=(t // tm,),
        in_specs=[row] + [full(a) for a in ws],
        out_specs=out_specs,
        out_shape=out_shape,
        compiler_params=_cp(1),
        name="odd_in_proj",
    )(x, *ws)


def _res_proj_kernel(x_ref, a_ref, w_ref, o_ref):
    o_ref[...] = x_ref[...] + _dot(a_ref[...], w_ref[...])


def _res_proj(x, a, w):
    t = x.shape[0]
    tm = _row_tile(t)
    return pl.pallas_call(
        _res_proj_kernel,
        grid=(t // tm,),
        in_specs=[pl.BlockSpec((tm, D_MODEL), lambda i: (i, 0)),
                  pl.BlockSpec((tm, a.shape[1]), lambda i: (i, 0)),
                  pl.BlockSpec(w.shape, lambda i: (0, 0))],
        out_specs=pl.BlockSpec((tm, D_MODEL), lambda i: (i, 0)),
        out_shape=jax.ShapeDtypeStruct((t, D_MODEL), F32),
        compiler_params=_cp(1),
        name="odd_out_proj",
    )(x, a, w)


def _router_kernel(x_ref, g_ref, w_ref, b_ref, xn_o, comb_o):
    xn = _rms(x_ref[...]) * g_ref[...]
    xn_o[...] = xn.astype(BF16)
    w = w_ref[...]
    w_hi = w.astype(BF16)
    w_lo = (w - w_hi.astype(F32)).astype(BF16)
    x_hi = xn.astype(BF16)
    x_lo = (xn - x_hi.astype(F32)).astype(BF16)
    logits = _dot(x_hi, w_hi) + _dot(x_hi, w_lo) + _dot(x_lo, w_hi) + b_ref[...]
    lane = lax.broadcasted_iota(jnp.int32, logits.shape, 1)
    logits = jnp.where(lane < N_EXPERTS, logits, -jnp.inf)
    v1 = jnp.max(logits, axis=1, keepdims=True)
    i1 = jnp.min(jnp.where(logits == v1, lane, LANES), axis=1, keepdims=True)
    rest = jnp.where(lane == i1, -jnp.inf, logits)
    v2 = jnp.max(rest, axis=1, keepdims=True)
    i2 = jnp.min(jnp.where(rest == v2, lane, LANES), axis=1, keepdims=True)
    e2 = jnp.exp(v2 - v1)
    g1 = 1.0 / (1.0 + e2)
    g2 = e2 * g1
    comb_o[...] = jnp.where(lane == i1, g1, 0.0) + jnp.where(lane == i2, g2, 0.0)


def _router(x, g, w, b):
    t = x.shape[0]
    tm = _row_tile(t)
    return pl.pallas_call(
        _router_kernel,
        grid=(t // tm,),
        in_specs=[pl.BlockSpec((tm, D_MODEL), lambda i: (i, 0)),
                  pl.BlockSpec((1, D_MODEL), lambda i: (0, 0)),
                  pl.BlockSpec((D_MODEL, LANES), lambda i: (0, 0)),
                  pl.BlockSpec((1, LANES), lambda i: (0, 0))],
        out_specs=[pl.BlockSpec((tm, D_MODEL), lambda i: (i, 0)), pl.BlockSpec((tm, LANES), lambda i: (i, 0))],
        out_shape=[jax.ShapeDtypeStruct((t, D_MODEL), BF16), jax.ShapeDtypeStruct((t, LANES), F32)],
        compiler_params=_cp(1),
        name="moe_router",
    )(x, g, w, b)


def _moe_kernel(y_ref, xn_ref, comb_ref, wg, wu, wd, o_ref, acc_s):
    e = pl.program_id(1)
    f = pl.program_id(2)

    @pl.when((e == 0) & (f == 0))
    def _():
        acc_s[...] = jnp.zeros_like(acc_s)

    xn = xn_ref[...]
    a = _dot(xn, wg[0])
    b = _dot(xn, wu[0])
    h = (a * jax.nn.sigmoid(a) * b).astype(BF16)
    comb = comb_ref[...]
    lane = lax.broadcasted_iota(jnp.int32, comb.shape, 1)
    gate = jnp.sum(jnp.where(lane == e, comb, 0.0), axis=1, keepdims=True)
    acc_s[...] += gate * _dot(h, wd[0])

    @pl.when((e == pl.num_programs(1) - 1) & (f == pl.num_programs(2) - 1))
    def _():
        o_ref[...] = y_ref[...] + acc_s[...]


def _moe(y, xn, comb, wg, wu, wd, tf):
    t = y.shape[0]
    tm = _row_tile(t)
    n_e, _, d_ff = wg.shape
    return pl.pallas_call(
        _moe_kernel,
        grid=(t // tm, n_e, d_ff // tf),
        in_specs=[pl.BlockSpec((tm, D_MODEL), lambda i, e, f: (i, 0)),
                  pl.BlockSpec((tm, D_MODEL), lambda i, e, f: (i, 0)),
                  pl.BlockSpec((tm, LANES), lambda i, e, f: (i, 0)),
                  pl.BlockSpec((1, D_MODEL, tf), lambda i, e, f: (e, 0, f)),
                  pl.BlockSpec((1, D_MODEL, tf), lambda i, e, f: (e, 0, f)),
                  pl.BlockSpec((1, tf, D_MODEL), lambda i, e, f: (e, f, 0))],
        out_specs=pl.BlockSpec((tm, D_MODEL), lambda i, e, f: (i, 0)),
        out_shape=jax.ShapeDtypeStruct((t, D_MODEL), F32),
        scratch_shapes=[pltpu.VMEM((tm, D_MODEL), F32)],
        compiler_params=_cp(3),
        name="moe_experts",
    )(y, xn, comb, wg, wu, wd)


def _fox_sample_kernel(pt_ref, q_ref, kn_ref, vn_ref, lfn_ref, *refs, n_pages):
    k_refs = refs[:n_pages]
    v_refs = refs[n_pages:2 * n_pages]
    lf_refs = refs[2 * n_pages:3 * n_pages]
    o_ref = refs[3 * n_pages]
    q = q_ref[0].astype(F32)
    qb = _lane_bcast_cols(q)
    sub = lax.broadcasted_iota(jnp.int32, (FOX_HEADS, FOX_W), 0)
    lane = lax.broadcasted_iota(jnp.int32, (FOX_HEADS, FOX_W), 1)
    sel = (lane >> 6) == sub
    r = lax.broadcasted_iota(jnp.int32, (PAGE_SIZE, PAGE_SIZE), 0)
    c = lax.broadcasted_iota(jnp.int32, (PAGE_SIZE, PAGE_SIZE), 1)
    later = jnp.where(r > c, 1.0, 0.0).astype(BF16)
    lf_new = lfn_ref[0]
    tail = jnp.zeros((FOX_HEADS, 1), F32)
    scores = [None] * n_pages
    for p in reversed(range(n_pages)):
        lf = lf_refs[p][0]
        bias = _dot_split3(lf, later) + tail + lf_new
        tail = tail + jnp.sum(lf, axis=1, keepdims=True)
        scores[p] = _slab_scores(k_refs[p], qb, range(FOX_HEADS)) + bias
    s_new = jnp.sum(jnp.where(sel, q * kn_ref[0], 0.0), axis=1, keepdims=True)
    m = s_new
    for s in scores:
        m = jnp.maximum(m, jnp.max(s, axis=1, keepdims=True))
    p_new = jnp.exp(s_new - m)
    l = p_new
    probs = [jnp.exp(s - m) for s in scores]
    for pr in probs:
        l = l + jnp.sum(pr, axis=1, keepdims=True)
    accs = []
    for h in range(FOX_HEADS):
        acc = jnp.zeros((FOX_HD, PAGE_SIZE), F32)
        for p in range(n_pages):
            acc = acc + v_refs[p][0, h] * probs[p][h:h + 1, :]
        accs.append(acc)
    o_row = jnp.sum(jnp.concatenate(accs, axis=0).T, axis=0, keepdims=True)
    inv_row = jnp.sum(jnp.where(sel, 1.0 / l, 0.0), axis=0, keepdims=True)
    new_row = jnp.sum(jnp.where(sel, p_new, 0.0), axis=0, keepdims=True)
    o_ref[0] = ((o_row + new_row * vn_ref[0]) * inv_row).astype(o_ref.dtype)


def _lane_bcast_cols(row):
    return jnp.broadcast_to(row, (LANES, row.shape[1])).T


def _slab_scores(kt_ref, qb, order):
    rows = []
    for r in order:
        prod = kt_ref[0, r] * qb[r * FOX_HD:(r + 1) * FOX_HD]
        rows.append(jnp.sum(prod, axis=0, keepdims=True))
    return jnp.concatenate(rows, axis=0)


def _fox_sample(page_table, q, k_new, v_new, lf_new, k_pool, v_pool, lf_pool_t):
    n_seq, n_pages = page_table.shape
    vec = lambda w: pl.BlockSpec((1, 1, w), lambda b, pt: (b, 0, 0))
    page = lambda shape, p: pl.BlockSpec((1,) + shape, lambda b, pt, p=p: (pt[b, p],) + (0,) * len(shape))
    in_specs = [vec(FOX_W), vec(FOX_W), vec(FOX_W), pl.BlockSpec((1, FOX_HEADS, 1), lambda b, pt: (b, 0, 0))]
    in_specs += [page((FOX_HEADS, FOX_HD, PAGE_SIZE), p) for p in range(n_pages)]
    in_specs += [page((FOX_HEADS, FOX_HD, PAGE_SIZE), p) for p in range(n_pages)]
    in_specs += [page((FOX_HEADS, PAGE_SIZE), p) for p in range(n_pages)]
    return pl.pallas_call(
        functools.partial(_fox_sample_kernel, n_pages=n_pages),
        grid_spec=pltpu.PrefetchScalarGridSpec(
            num_scalar_prefetch=1, grid=(n_seq,), in_specs=in_specs, out_specs=vec(FOX_W)),
        out_shape=jax.ShapeDtypeStruct((n_seq, 1, FOX_W), BF16),
        compiler_params=_cp(1),
        name="fox_sample_attn",
    )(page_table, q, k_new, v_new, lf_new, *([k_pool] * n_pages), *([v_pool] * n_pages),
      *([lf_pool_t] * n_pages))


def _diff_sample_kernel(pt_ref, q_ref, kn_ref, vn_ref, lam_ref, gain_ref, rep_ref, *refs, n_pages):
    k_refs = refs[:n_pages]
    v_refs = refs[n_pages:2 * n_pages]
    o_ref = refs[2 * n_pages]
    n_maps = 2 * DIFF_HEADS
    q = q_ref[0].astype(F32)
    qb = _lane_bcast_cols(q)
    sub = lax.broadcasted_iota(jnp.int32, (n_maps, DIFF_W), 0)
    lane = lax.broadcasted_iota(jnp.int32, (n_maps, DIFF_W), 1)
    sel = (lane >> 6) == ((sub & (DIFF_HEADS - 1)) * 2 + (sub >> 3))
    order = [2 * h + c for c in range(2) for h in range(DIFF_HEADS)]
    scores = [_slab_scores(k_refs[p], qb, order) for p in range(n_pages)]
    s_new = jnp.sum(jnp.where(sel, q * kn_ref[0], 0.0), axis=1, keepdims=True)
    m = s_new
    for s in scores:
        m = jnp.maximum(m, jnp.max(s, axis=1, keepdims=True))
    probs = [jnp.exp(s - m) for s in scores]
    p_new = jnp.exp(s_new - m)
    l = p_new
    for pr in probs:
        l = l + jnp.sum(pr, axis=1, keepdims=True)
    inv_l = 1.0 / l
    lam = _diff_lambda(lam_ref)

    def mix(pr):
        pn = pr * inv_l
        return pn[:DIFF_HEADS] - lam * pn[DIFF_HEADS:]

    sub8 = lax.broadcasted_iota(jnp.int32, (DIFF_HEADS, DIFF_HEADS * PAGE_SIZE), 0)
    lane8 = lax.broadcasted_iota(jnp.int32, (DIFF_HEADS, DIFF_HEADS * PAGE_SIZE), 1)
    own = (lane8 & (DIFF_HEADS - 1)) == sub8
    acc = mix(p_new) * vn_ref[0]
    for p in range(n_pages):
        spread = _dot(mix(probs[p]).astype(BF16), rep_ref[...])
        w_rows = jnp.where(own, spread, 0.0).astype(BF16)
        v_rows = v_refs[p][0].reshape(DIFF_HEADS * PAGE_SIZE, 2 * DIFF_HD).astype(BF16)
        acc = acc + _dot(w_rows, v_rows)
    o = _rms(acc) * gain_ref[...] * (1.0 - LAM_INIT)
    o_ref[0] = o.astype(o_ref.dtype)


def _diff_sample(page_table, q, k_new, v_new, lam_vecs, gain, k_pool, v_pool):
    n_seq, n_pages = page_table.shape
    vec = lambda w: pl.BlockSpec((1, 1, w), lambda b, pt: (b, 0, 0))
    head_rows = pl.BlockSpec((1, DIFF_HEADS, 2 * DIFF_HD), lambda b, pt: (b, 0, 0))
    page = lambda shape, p: pl.BlockSpec((1,) + shape, lambda b, pt, p=p: (pt[b, p], 0, 0, 0))
    i = jnp.arange(DIFF_HEADS * PAGE_SIZE)
    rep = (i[None, :] // DIFF_HEADS == jnp.arange(PAGE_SIZE)[:, None]).astype(BF16)
    in_specs = [vec(DIFF_W), vec(DIFF_W), head_rows,
                pl.BlockSpec(lam_vecs.shape, lambda b, pt: (0, 0)),
                pl.BlockSpec(gain.shape, lambda b, pt: (0, 0)),
                pl.BlockSpec(rep.shape, lambda b, pt: (0, 0))]
    in_specs += [page((2 * DIFF_HEADS, DIFF_HD, PAGE_SIZE), p) for p in range(n_pages)]
    in_specs += [page((PAGE_SIZE, DIFF_HEADS, 2 * DIFF_HD), p) for p in range(n_pages)]
    return pl.pallas_call(
        functools.partial(_diff_sample_kernel, n_pages=n_pages),
        grid_spec=pltpu.PrefetchScalarGridSpec(
            num_scalar_prefetch=1, grid=(n_seq,), in_specs=in_specs, out_specs=head_rows),
        out_shape=jax.ShapeDtypeStruct((n_seq, DIFF_HEADS, 2 * DIFF_HD), BF16),
        compiler_params=_cp(1, 60 * 1024 * 1024),
        name="diff_sample_attn",
    )(page_table, q, k_new, v_new, lam_vecs, gain, rep, *([k_pool] * n_pages), *([v_pool] * n_pages))


def _seg_matrix():
    i = jnp.arange(MXU_DIM)
    return jnp.where((i[:, None] // FOX_HD) == (i[None, :] // FOX_HD), 1.0 / FOX_HD, 0.0).astype(BF16)


def kernel(x_prompt, x_sample, cache_fox_k, cache_fox_v, cache_fox_logf, cache_diff_k, cache_diff_v, page_table,
           even_norm_mix, even_w_in, even_b_in, fox_q_gain, fox_k_gain, sgu_ln_gain, sgu_ln_bias, sgu_w_s, sgu_b_s,
           even_w_out, even_norm_ffn, ffn_w_gate, ffn_w_up, ffn_w_down,
           odd_norm_mix, odd_w_in, diff_q_gain, diff_k_gain, diff_lambda_q1, diff_lambda_k1, diff_lambda_q2,
           diff_lambda_k2, diff_subln_gain, odd_w_out, odd_norm_ffn, router_w, router_b,
           moe_w_gate, moe_w_up, moe_w_down):
    batch, seq, _ = x_prompt.shape
    n_seq = x_sample.shape[0]
    n_phys = cache_fox_k.shape[1]
    seg = _seg_matrix()
    row = lambda a: a.reshape(1, -1).astype(F32)

    w_in, b_in = even_w_in[0], even_b_in[0]
    o_f = 3 * FOX_W
    o_u = o_f + FOX_HEADS
    o_vg = o_u + SGU_W
    pad_f = LANES - FOX_HEADS
    pe = dict(
        g_mix=row(even_norm_mix[0]),
        wq=w_in[:, :FOX_W].astype(BF16), wk=w_in[:, FOX_W:2 * FOX_W].astype(BF16),
        wv=w_in[:, 2 * FOX_W:o_f].astype(BF16),
        wf=jnp.pad(w_in[:, o_f:o_u], ((0, 0), (0, pad_f))).astype(BF16),
        wu=w_in[:, o_u:o_vg].astype(BF16), wvg=w_in[:, o_vg:].astype(BF16),
        bq=row(b_in[:FOX_W]), bk=row(b_in[FOX_W:2 * FOX_W]), bv=row(b_in[2 * FOX_W:o_f]),
        bf=row(jnp.pad(b_in[o_f:o_u], (0, pad_f))), bu=row(b_in[o_u:o_vg]), bvg=row(b_in[o_vg:]),
        qg=row(jnp.tile(fox_q_gain[0], FOX_HEADS)), kg=row(jnp.tile(fox_k_gain[0], FOX_HEADS)),
        lng=row(sgu_ln_gain[0]), lnb=row(sgu_ln_bias[0]), seg=seg)
    w_s = sgu_w_s[0]
    b_s_full = jnp.broadcast_to(sgu_b_s[0][:, :, None], (SGU_GROUPS, CHUNK, SGU_CH)).astype(F32)
    w_s_first = row(jnp.repeat(w_s[:, 0, 0], SGU_CH))
    b_s_first = row(jnp.repeat(sgu_b_s[0][:, 0], SGU_CH))
    w_out1 = even_w_out[0][:FOX_W].astype(BF16)
    w_out2 = even_w_out[0][FOX_W:].astype(BF16)
    g_ffn = row(even_norm_ffn[0])
    ffn_g, ffn_u, ffn_d = ffn_w_gate[0].astype(BF16), ffn_w_up[0].astype(BF16), ffn_w_down[0].astype(BF16)

    wo_in = odd_w_in[0]
    po = dict(
        g_mix=row(odd_norm_mix[0]),
        wq=wo_in[:, :DIFF_W].astype(BF16), wk=wo_in[:, DIFF_W:2 * DIFF_W].astype(BF16),
        wv=wo_in[:, 2 * DIFF_W:].astype(BF16),
        qg=row(jnp.tile(diff_q_gain[0].reshape(-1), DIFF_HEADS)),
        kg=row(jnp.tile(diff_k_gain[0].reshape(-1), DIFF_HEADS)), seg=seg)
    lam_vecs = jnp.stack([diff_lambda_q1[0], diff_lambda_k1[0], diff_lambda_q2[0], diff_lambda_k2[0]]).astype(F32)
    subln = row(diff_subln_gain[0])
    subln_t = row(jnp.tile(diff_subln_gain[0], DIFF_HEADS))
    w_oo = odd_w_out[0].astype(BF16)
    g_moe = row(odd_norm_ffn[0])
    r_w = jnp.pad(router_w[0], ((0, 0), (0, LANES - N_EXPERTS))).astype(F32)
    r_b = row(jnp.pad(router_b[0], (0, LANES - N_EXPERTS)))
    moe_g, moe_u, moe_d = moe_w_gate[0].astype(BF16), moe_w_up[0].astype(BF16), moe_w_down[0].astype(BF16)

    xp = x_prompt.reshape(batch * seq, D_MODEL)
    xs = x_sample.reshape(n_seq, D_MODEL)

    def rows_out(a_t, *feat):
        b, _, s = a_t.shape
        n = len(feat)
        a = a_t.reshape((b,) + feat + (s,)).transpose((0, n + 1) + tuple(range(1, n + 1)))
        return a.reshape((1, b, s) + feat)

    q, fkt_p, fvt_p, lft_p, u, vn, kb, vb = _even_in(xp, pe, batch, seq, sample=False)
    tq = 512
    c5 = _cumsum_rows(lft_p.reshape(batch * FOX_HEADS, seq)).reshape(batch, FOX_HEADS // 2, 2, seq // tq, tq)
    fo = _pair_attn(q, kb, vb, batch, seq, (c5,), fox=True, tq=tq)
    yp = _even_out(xp, fo, u, vn, w_s, b_s_full, w_out1, w_out2, single=False)
    yp = _swiglu(yp, g_ffn, ffn_g, ffn_u, ffn_d, tf=1408)

    qs, fkt_s, fvt_s, lft_s, us, vn_s, fk_s, fv_s, lf_s = _even_in(xs, pe, 1, n_seq, sample=True)
    fos = _fox_sample(
        page_table, qs.reshape(n_seq, 1, FOX_W), fk_s.reshape(n_seq, 1, FOX_W), fv_s.reshape(n_seq, 1, FOX_W),
        lf_s.reshape(n_seq, FOX_HEADS, 1),
        cache_fox_k[0].transpose(0, 2, 3, 1), cache_fox_v[0].transpose(0, 2, 3, 1),
        cache_fox_logf[0].transpose(0, 2, 1))
    ys = _even_out(xs, fos.reshape(n_seq, FOX_W), us, vn_s, w_s_first, b_s_first, w_out1, w_out2, single=True)
    ys = _swiglu(ys, g_ffn, ffn_g, ffn_u, ffn_d, tf=1408)

    q, dkt_p, dv_p, kb, vb = _odd_in(yp, po, batch, seq, sample=False)
    do = _pair_attn(q, kb, vb, batch, seq, (lam_vecs, subln), fox=False, tq=tq)
    yp = _res_proj(yp, do, w_oo)
    xn, comb = _router(yp, g_moe, r_w, r_b)
    yp = _moe(yp, xn, comb, moe_g, moe_u, moe_d, tf=896)

    qs, dkt_s, dv_s, dk_s = _odd_in(ys, po, 1, n_seq, sample=True)
    dos = _diff_sample(
        page_table, qs.reshape(n_seq, 1, DIFF_W), dk_s.reshape(n_seq, 1, DIFF_W),
        dv_s.reshape(n_seq, DIFF_HEADS, 2 * DIFF_HD), lam_vecs, subln,
        cache_diff_k[0].transpose(0, 2, 3, 4, 1).reshape(n_phys, 2 * DIFF_HEADS, DIFF_HD, PAGE_SIZE),
        cache_diff_v[0])
    ys = _res_proj(ys, dos.reshape(n_seq, DIFF_W), w_oo)
    xn, comb = _router(ys, g_moe, r_w, r_b)
    ys = _moe(ys, xn, comb, moe_g, moe_u, moe_d, tf=896)

    dec = x_sample.shape[1]
    sample_rows = lambda a_t, *feat: rows_out(a_t, *feat).reshape((1, n_seq, dec) + feat)
    return (yp.reshape(batch, seq, D_MODEL), ys.reshape(n_seq, dec, D_MODEL),
            rows_out(fkt_p, FOX_HEADS, FOX_HD), rows_out(fvt_p, FOX_HEADS, FOX_HD), rows_out(lft_p, FOX_HEADS),
            rows_out(dkt_p, DIFF_HEADS, 2, DIFF_HD), dv_p.reshape(1, batch, seq, DIFF_HEADS, 2 * DIFF_HD),
            sample_rows(fkt_s, FOX_HEADS, FOX_HD), sample_rows(fvt_s, FOX_HEADS, FOX_HD),
            sample_rows(lft_s, FOX_HEADS),
            sample_rows(dkt_s, DIFF_HEADS, 2, DIFF_HD), dv_s.reshape(1, n_seq, dec, DIFF_HEADS, 2 * DIFF_HD),
            vn_s.reshape(1, n_seq, dec, SGU_GROUPS, SGU_CH))
```

```python
import functools
import math

import jax
import jax.numpy as jnp
from jax import lax
from jax.experimental import pallas as pl
from jax.experimental.pallas import tpu as pltpu

F32 = jnp.float32
BF16 = jnp.bfloat16

D_MODEL = 1024
FOX_HEADS = 8
FOX_HD = 64
FOX_W = 512
SGU_GROUPS = 4
SGU_CH = 128
SGU_W = 512
CHUNK = 128
DIFF_HEADS = 8
DIFF_HD = 64
DIFF_W = 1024
N_EXPERTS = 8
PAGE_SIZE = 128
NORM_EPS = 1e-6
FOX_SCALE = FOX_HD ** -0.5
DIFF_SCALE = DIFF_HD ** -0.5
LAM_INIT = 0.8 - 0.6 * math.exp(-0.3 * 1)

LANES = 128
MXU_DIM = 256
VMEM_LIMIT = 52 * 1024 * 1024


def _cp(n_axes, vmem=VMEM_LIMIT):
    return pltpu.CompilerParams(dimension_semantics=("arbitrary",) * n_axes, vmem_limit_bytes=vmem)


def _dot(a, b):
    return jnp.dot(a, b, preferred_element_type=F32)


def _dot_nt(a, b):
    return lax.dot_general(a, b, (((1,), (1,)), ((), ())), preferred_element_type=F32)


def _dot_split3(x, m_bf16):
    hi = x.astype(BF16)
    r1 = x - hi.astype(F32)
    mid = r1.astype(BF16)
    lo = (r1 - mid.astype(F32)).astype(BF16)
    return _dot(hi, m_bf16) + _dot(mid, m_bf16) + _dot(lo, m_bf16)


def _rms(x):
    return x * lax.rsqrt(jnp.mean(x * x, axis=-1, keepdims=True) + NORM_EPS)


def _gelu(x):
    return 0.5 * x * (1.0 + lax.erf(x * (2.0 ** -0.5)))


def _seg_rms(z, seg_ref, gain):
    outs = []
    for j in range(z.shape[1] // MXU_DIM):
        zj = z[:, j * MXU_DIM:(j + 1) * MXU_DIM]
        ms = _dot((zj * zj).astype(BF16), seg_ref[...])
        outs.append(zj * lax.rsqrt(ms + NORM_EPS))
    return jnp.concatenate(outs, axis=1) * gain


def _row_tile(t):
    return 512 if t % 512 == 0 else t


def _even_in_kernel(x_ref, g_ref, wq, wk, wv, wf, wu, wvg, bq, bk, bv, bf, bu, bvg, qg, kg, lng, lnb, seg,
                    q_o, kt_o, vt_o, lft_o, u_o, vn_o, *extra, sample):
    xn = (_rms(x_ref[...]) * g_ref[...]).astype(BF16)

    def proj(w, b):
        return _dot(xn, w[...]) + b[...]

    q = _seg_rms(proj(wq, bq), seg, qg[...])
    q_o[...] = (q * FOX_SCALE).astype(BF16)
    k = _seg_rms(proj(wk, bk), seg, kg[...])
    kt_o[0] = k.T
    v = proj(wv, bv)
    vt_o[0] = v.T
    lf = jax.nn.log_sigmoid(proj(wf, bf))
    lft_o[0] = lf.T[:FOX_HEADS]
    if sample:
        k_o, v_o, lf_o = extra
        k_o[...] = k
        v_o[...] = v
        lf_o[...] = lf[:, :FOX_HEADS]
    else:
        kb_o, vb_o = extra
        kb_o[...] = k.astype(BF16)
        vb_o[...] = v.astype(BF16)
    u_o[...] = _gelu(proj(wu, bu)).astype(BF16)
    vg = _gelu(proj(wvg, bvg))
    parts = []
    for gi in range(SGU_GROUPS):
        c = vg[:, gi * SGU_CH:(gi + 1) * SGU_CH]
        xc = c - jnp.mean(c, axis=-1, keepdims=True)
        parts.append(xc * lax.rsqrt(jnp.mean(xc * xc, axis=-1, keepdims=True) + NORM_EPS))
    vn = jnp.concatenate(parts, axis=1) * lng[...] + lnb[...]
    vn_o[...] = vn.astype(vn_o.dtype)


def _even_in(x, p, batch, seq, sample):
    t = x.shape[0]
    tm = _row_tile(seq)
    n_s = seq // tm
    row = lambda w: pl.BlockSpec((tm, w), lambda i: (i, 0))
    col = lambda w: pl.BlockSpec((1, w, tm), lambda i: (i // n_s, 0, i % n_s))
    full = lambda a: pl.BlockSpec(a.shape, lambda i: (0,) * a.ndim)
    ws = [p["wq"], p["wk"], p["wv"], p["wf"], p["wu"], p["wvg"], p["bq"], p["bk"], p["bv"], p["bf"], p["bu"],
          p["bvg"], p["qg"], p["kg"], p["lng"], p["lnb"], p["seg"]]
    nat = lambda w, dt: jax.ShapeDtypeStruct((t, w), dt)
    tr = lambda w: jax.ShapeDtypeStruct((batch, w, seq), F32)
    out_specs = [row(FOX_W), col(FOX_W), col(FOX_W), col(FOX_HEADS), row(SGU_W), row(SGU_W)]
    out_shape = [nat(FOX_W, BF16), tr(FOX_W), tr(FOX_W), tr(FOX_HEADS), nat(SGU_W, BF16),
                 nat(SGU_W, F32 if sample else BF16)]
    if sample:
        out_specs += [row(FOX_W), row(FOX_W), row(FOX_HEADS)]
        out_shape += [nat(FOX_W, F32), nat(FOX_W, F32), nat(FOX_HEADS, F32)]
    else:
        out_specs += [row(FOX_W), row(FOX_W)]
        out_shape += [nat(FOX_W, BF16), nat(FOX_W, BF16)]
    return pl.pallas_call(
        functools.partial(_even_in_kernel, sample=sample),
        grid=(t // tm,),
        in_specs=[row(D_MODEL), full(p["g_mix"])] + [full(a) for a in ws],
        out_specs=out_specs,
        out_shape=out_shape,
        compiler_params=_cp(1),
        name="even_in_proj",
    )(x, p["g_mix"], *ws)


def _cumsum_kernel(x_ref, o_ref):
    r = lax.broadcasted_iota(jnp.int32, (LANES, LANES), 0)
    c = lax.broadcasted_iota(jnp.int32, (LANES, LANES), 1)
    upper = jnp.where(r <= c, 1.0, 0.0).astype(BF16)
    carry = jnp.zeros((x_ref.shape[0], 1), F32)
    for j in range(x_ref.shape[1] // LANES):
        blk = x_ref[:, j * LANES:(j + 1) * LANES]
        out = _dot_split3(blk, upper) + carry
        o_ref[:, j * LANES:(j + 1) * LANES] = out
        carry = out[:, LANES - 1:LANES]


def _cumsum_rows(x):
    return pl.pallas_call(
        _cumsum_kernel,
        out_shape=jax.ShapeDtypeStruct(x.shape, F32),
        name="logf_cumsum",
    )(x)


def _pair_attn_kernel(*refs, fox, tq):
    if fox:
        q_ref, k_ref, v_ref, c_ref, o_ref = refs
    else:
        q_ref, k_ref, v_ref, lam_ref, gain_ref, o_ref = refs
    qi = pl.program_id(2)
    q = q_ref[...]
    lane = lax.broadcasted_iota(jnp.int32, (1, LANES), 1)
    zero = jnp.zeros_like(q)
    q_maps = (jnp.where(lane < FOX_HD, q, zero), jnp.where(lane >= FOX_HD, q, zero))
    row = lax.broadcasted_iota(jnp.int32, (tq, tq), 0)
    col = lax.broadcasted_iota(jnp.int32, (tq, tq), 1)

    def step(kj, carry, masked):
        start = pl.multiple_of(kj * tq, tq)
        k_t = k_ref[pl.ds(start, tq), :]
        v_t = v_ref[pl.ds(start, tq), :]
        new = []
        for h in range(2):
            m, l, acc = carry[h]
            s = _dot_nt(q_maps[h], k_t)
            if fox:
                s = s - c_ref[0, 0, h, pl.ds(kj, 1), :]
            if masked:
                s = jnp.where(row >= col, s, -jnp.inf)
            m_new = jnp.maximum(m, jnp.max(s, axis=1, keepdims=True))
            alpha = jnp.exp(m - m_new)
            p = jnp.exp(s - m_new)
            l = alpha * l + jnp.sum(p, axis=1, keepdims=True)
            acc = alpha * acc + _dot(p.astype(BF16), v_t)
            new.append((m_new, l, acc))
        return tuple(new)

    init = tuple((jnp.full((tq, 1), -jnp.inf, F32), jnp.zeros((tq, 1), F32), jnp.zeros((tq, LANES), F32))
                 for _ in range(2))
    carry = lax.fori_loop(0, qi, lambda kj, c: step(kj, c, False), init)
    (_, l_a, acc_a), (_, l_b, acc_b) = step(qi, carry, True)
    o_a = acc_a * (1.0 / l_a)
    o_b = acc_b * (1.0 / l_b)
    if fox:
        o_ref[...] = jnp.where(lane < FOX_HD, o_a, o_b).astype(o_ref.dtype)
    else:
        lam = _diff_lambda(lam_ref)
        o = o_a - lam * o_b
        o = _rms(o) * gain_ref[...] * (1.0 - LAM_INIT)
        o_ref[...] = o.astype(o_ref.dtype)


def _diff_lambda(lam_ref):
    a = jnp.sum(lam_ref[0:1, :] * lam_ref[1:2, :], axis=1, keepdims=True)
    b = jnp.sum(lam_ref[2:3, :] * lam_ref[3:4, :], axis=1, keepdims=True)
    return jnp.exp(a) - jnp.exp(b) + LAM_INIT


def _pair_attn(q, k, v, batch, seq, extra, fox, tq=512):
    width = q.shape[1]
    n_pairs = width // LANES
    nq = seq // tq
    q_spec = pl.BlockSpec((tq, LANES), lambda b, h, i: (b * nq + i, h))
    kv_spec = pl.BlockSpec((seq, LANES), lambda b, h, i: (b, h))
    if fox:
        (c5,) = extra
        extra_specs = [pl.BlockSpec((1, 1, 2, nq, tq), lambda b, h, i: (b, h, 0, 0, 0))]
    else:
        extra_specs = [pl.BlockSpec(a.shape, lambda b, h, i: (0, 0)) for a in extra]
    return pl.pallas_call(
        functools.partial(_pair_attn_kernel, fox=fox, tq=tq),
        grid=(batch, n_pairs, nq),
        in_specs=[q_spec, kv_spec, kv_spec] + extra_specs,
        out_specs=q_spec,
        out_shape=jax.ShapeDtypeStruct(q.shape, BF16),
        compiler_params=_cp(3),
        name="fox_prompt_attn" if fox else "diff_prompt_attn",
    )(q, k, v, *extra)


def _even_out_kernel(x_ref, fo_ref, u_ref, vn_ref, ws_ref, bs_ref, w1, w2, o_ref, *, single):
    tm = x_ref.shape[0]
    u = u_ref[...].astype(F32)
    if single:
        so = u * (vn_ref[...].astype(F32) * ws_ref[...] + bs_ref[...])
    else:
        r = lax.broadcasted_iota(jnp.int32, (CHUNK, CHUNK), 0)
        c = lax.broadcasted_iota(jnp.int32, (CHUNK, CHUNK), 1)
        tril = [jnp.where(r >= c, ws_ref[g], 0.0).astype(BF16) for g in range(SGU_GROUPS)]
        rows = []
        for ch in range(tm // CHUNK):
            rs = slice(ch * CHUNK, (ch + 1) * CHUNK)
            cols = []
            for g in range(SGU_GROUPS):
                cs = slice(g * SGU_CH, (g + 1) * SGU_CH)
                mixed = _dot(tril[g], vn_ref[rs, cs]) + bs_ref[g]
                cols.append(u[rs, cs] * mixed)
            rows.append(jnp.concatenate(cols, axis=1))
        so = jnp.concatenate(rows, axis=0)
    o_ref[...] = x_ref[...] + _dot(fo_ref[...], w1[...]) + _dot(so.astype(BF16), w2[...])


def _even_out(x, fo, u, vn, ws, bs, w1, w2, single):
    t = x.shape[0]
    tm = _row_tile(t)
    row = lambda w: pl.BlockSpec((tm, w), lambda i: (i, 0))
    full = lambda a: pl.BlockSpec(a.shape, lambda i: (0,) * a.ndim)
    return pl.pallas_call(
        functools.partial(_even_out_kernel, single=single),
        grid=(t // tm,),
        in_specs=[row(D_MODEL), row(FOX_W), row(SGU_W), row(SGU_W), full(ws), full(bs), full(w1), full(w2)],
        out_specs=row(D_MODEL),
        out_shape=jax.ShapeDtypeStruct((t, D_MODEL), F32),
        compiler_params=_cp(1),
        name="even_out_proj",
    )(x, fo, u, vn, ws, bs, w1, w2)


def _swiglu_kernel(x_ref, g_ref, wg, wu, wd, o_ref, xn_s, acc_s):
    f = pl.program_id(1)

    @pl.when(f == 0)
    def _():
        xn_s[...] = (_rms(x_ref[...]) * g_ref[...]).astype(BF16)
        acc_s[...] = jnp.zeros_like(acc_s)

    xn = xn_s[...]
    a = _dot(xn, wg[...])
    b = _dot(xn, wu[...])
    h = (a * jax.nn.sigmoid(a) * b).astype(BF16)
    acc_s[...] += _dot(h, wd[...])

    @pl.when(f == pl.num_programs(1) - 1)
    def _():
        o_ref[...] = x_ref[...] + acc_s[...]


def _swiglu(x, g, wg, wu, wd, tf):
    t = x.shape[0]
    tm = _row_tile(t)
    d_ff = wg.shape[1]
    return pl.pallas_call(
        _swiglu_kernel,
        grid=(t // tm, d_ff // tf),
        in_specs=[pl.BlockSpec((tm, D_MODEL), lambda i, f: (i, 0)),
                  pl.BlockSpec((1, D_MODEL), lambda i, f: (0, 0)),
                  pl.BlockSpec((D_MODEL, tf), lambda i, f: (0, f)),
                  pl.BlockSpec((D_MODEL, tf), lambda i, f: (0, f)),
                  pl.BlockSpec((tf, D_MODEL), lambda i, f: (f, 0))],
        out_specs=pl.BlockSpec((tm, D_MODEL), lambda i, f: (i, 0)),
        out_shape=jax.ShapeDtypeStruct((t, D_MODEL), F32),
        scratch_shapes=[pltpu.VMEM((tm, D_MODEL), BF16), pltpu.VMEM((tm, D_MODEL), F32)],
        compiler_params=_cp(2),
        name="dense_swiglu",
    )(x, g, wg, wu, wd)


def _odd_in_kernel(x_ref, g_ref, wq, wk, wv, qg, kg, seg, q_o, kt_o, v_o, *extra, sample):
    xn = (_rms(x_ref[...]) * g_ref[...]).astype(BF16)
    q = _seg_rms(_dot(xn, wq[...]), seg, qg[...])
    q_o[...] = (q * DIFF_SCALE).astype(BF16)
    k = _seg_rms(_dot(xn, wk[...]), seg, kg[...])
    kt_o[0] = k.T
    v = _dot(xn, wv[...])
    v_o[...] = v
    if sample:
        (k_o,) = extra
        k_o[...] = k
    else:
        kb_o, vb_o = extra
        kb_o[...] = k.astype(BF16)
        vb_o[...] = v.astype(BF16)


def _odd_in(x, p, batch, seq, sample):
    t = x.shape[0]
    tm = _row_tile(seq)
    n_s = seq // tm
    row = pl.BlockSpec((tm, DIFF_W), lambda i: (i, 0))
    col = pl.BlockSpec((1, DIFF_W, tm), lambda i: (i // n_s, 0, i % n_s))
    full = lambda a: pl.BlockSpec(a.shape, lambda i: (0,) * a.ndim)
    ws = [p["g_mix"], p["wq"], p["wk"], p["wv"], p["qg"], p["kg"], p["seg"]]
    nat = lambda dt: jax.ShapeDtypeStruct((t, DIFF_W), dt)
    out_specs = [row, col, row]
    out_shape = [nat(BF16), jax.ShapeDtypeStruct((batch, DIFF_W, seq), F32), nat(F32)]
    if sample:
        out_specs += [row]
        out_shape += [nat(F32)]
    else:
        out_specs += [row, row]
        out_shape += [nat(BF16), nat(BF16)]
    return pl.pallas_call(
        functools.partial(_odd_in_kernel, sample=sample),
        grid=(t // tm,),
        in_specs=[row] + [full(a) for a in ws],
        out_specs=out_specs,
        out_shape=out_shape,
        compiler_params=_cp(1),
        name="odd_in_proj",
    )(x, *ws)


def _res_proj_kernel(x_ref, a_ref, w_ref, o_ref):
    o_ref[...] = x_ref[...] + _dot(a_ref[...], w_ref[...])


def _res_proj(x, a, w):
    t = x.shape[0]
    tm = _row_tile(t)
    return pl.pallas_call(
        _res_proj_kernel,
        grid=(t // tm,),
        in_specs=[pl.BlockSpec((tm, D_MODEL), lambda i: (i, 0)),
                  pl.BlockSpec((tm, a.shape[1]), lambda i: (i, 0)),
                  pl.BlockSpec(w.shape, lambda i: (0, 0))],
        out_specs=pl.BlockSpec((tm, D_MODEL), lambda i: (i, 0)),
        out_shape=jax.ShapeDtypeStruct((t, D_MODEL), F32),
        compiler_params=_cp(1),
        name="odd_out_proj",
    )(x, a, w)


def _router_kernel(x_ref, g_ref, w_ref, b_ref, xn_o, comb_o):
    xn = _rms(x_ref[...]) * g_ref[...]
    xn_o[...] = xn.astype(BF16)
    w = w_ref[...]
    w_hi = w.astype(BF16)
    w_lo = (w - w_hi.astype(F32)).astype(BF16)
    x_hi = xn.astype(BF16)
    x_lo = (xn - x_hi.astype(F32)).astype(BF16)
    logits = _dot(x_hi, w_hi) + _dot(x_hi, w_lo) + _dot(x_lo, w_hi) + b_ref[...]
    lane = lax.broadcasted_iota(jnp.int32, logits.shape, 1)
    logits = jnp.where(lane < N_EXPERTS, logits, -jnp.inf)
    v1 = jnp.max(logits, axis=1, keepdims=True)
    i1 = jnp.min(jnp.where(logits == v1, lane, LANES), axis=1, keepdims=True)
    rest = jnp.where(lane == i1, -jnp.inf, logits)
    v2 = jnp.max(rest, axis=1, keepdims=True)
    i2 = jnp.min(jnp.where(rest == v2, lane, LANES), axis=1, keepdims=True)
    e2 = jnp.exp(v2 - v1)
    g1 = 1.0 / (1.0 + e2)
    g2 = e2 * g1
    comb_o[...] = jnp.where(lane == i1, g1, 0.0) + jnp.where(lane == i2, g2, 0.0)


def _router(x, g, w, b):
    t = x.shape[0]
    tm = _row_tile(t)
    return pl.pallas_call(
        _router_kernel,
        grid=(t // tm,),
        in_specs=[pl.BlockSpec((tm, D_MODEL), lambda i: (i, 0)),
                  pl.BlockSpec((1, D_MODEL), lambda i: (0, 0)),
                  pl.BlockSpec((D_MODEL, LANES), lambda i: (0, 0)),
                  pl.BlockSpec((1, LANES), lambda i: (0, 0))],
        out_specs=[pl.BlockSpec((tm, D_MODEL), lambda i: (i, 0)), pl.BlockSpec((tm, LANES), lambda i: (i, 0))],
        out_shape=[jax.ShapeDtypeStruct((t, D_MODEL), BF16), jax.ShapeDtypeStruct((t, LANES), F32)],
        compiler_params=_cp(1),
        name="moe_router",
    )(x, g, w, b)


def _moe_kernel(y_ref, xn_ref, comb_ref, wg, wu, wd, o_ref, acc_s):
    e = pl.program_id(1)
    f = pl.program_id(2)

    @pl.when((e == 0) & (f == 0))
    def _():
        acc_s[...] = jnp.zeros_like(acc_s)

    xn = xn_ref[...]
    a = _dot(xn, wg[0])
    b = _dot(xn, wu[0])
    h = (a * jax.nn.sigmoid(a) * b).astype(BF16)
    comb = comb_ref[...]
    lane = lax.broadcasted_iota(jnp.int32, comb.shape, 1)
    gate = jnp.sum(jnp.where(lane == e, comb, 0.0), axis=1, keepdims=True)
    acc_s[...] += gate * _dot(h, wd[0])

    @pl.when((e == pl.num_programs(1) - 1) & (f == pl.num_programs(2) - 1))
    def _():
        o_ref[...] = y_ref[...] + acc_s[...]


def _moe(y, xn, comb, wg, wu, wd, tf):
    t = y.shape[0]
    tm = _row_tile(t)
    n_e, _, d_ff = wg.shape
    return pl.pallas_call(
        _moe_kernel,
        grid=(t // tm, n_e, d_ff // tf),
        in_specs=[pl.BlockSpec((tm, D_MODEL), lambda i, e, f: (i, 0)),
                  pl.BlockSpec((tm, D_MODEL), lambda i, e, f: (i, 0)),
                  pl.BlockSpec((tm, LANES), lambda i, e, f: (i, 0)),
                  pl.BlockSpec((1, D_MODEL, tf), lambda i, e, f: (e, 0, f)),
                  pl.BlockSpec((1, D_MODEL, tf), lambda i, e, f: (e, 0, f)),
                  pl.BlockSpec((1, tf, D_MODEL), lambda i, e, f: (e, f, 0))],
        out_specs=pl.BlockSpec((tm, D_MODEL), lambda i, e, f: (i, 0)),
        out_shape=jax.ShapeDtypeStruct((t, D_MODEL), F32),
        scratch_shapes=[pltpu.VMEM((tm, D_MODEL), F32)],
        compiler_params=_cp(3),
        name="moe_experts",
    )(y, xn, comb, wg, wu, wd)


MOE_TILE = 512
MOE_CHUNK = 512
MOE_WIN = 256
MOE_ALIGN = 16


def _route_kernel(x_ref, g_ref, w_ref, b_ref, xn_o, cols_o, rows_o, cnt_o, carry_s):
    @pl.when(pl.program_id(0) == 0)
    def _():
        carry_s[...] = jnp.zeros_like(carry_s)

    xn = _rms(x_ref[...]) * g_ref[...]
    xn_o[...] = xn.astype(BF16)
    w = w_ref[...]
    w_hi = w.astype(BF16)
    w_lo = (w - w_hi.astype(F32)).astype(BF16)
    x_hi = xn.astype(BF16)
    x_lo = (xn - x_hi.astype(F32)).astype(BF16)
    logits = _dot(x_hi, w_hi) + _dot(x_hi, w_lo) + _dot(x_lo, w_hi) + b_ref[...]
    tm = logits.shape[0]
    lane = lax.broadcasted_iota(jnp.int32, logits.shape, 1)
    logits = jnp.where(lane < N_EXPERTS, logits, -jnp.inf)
    v1 = jnp.max(logits, axis=1, keepdims=True)
    i1 = jnp.min(jnp.where(logits == v1, lane, LANES), axis=1, keepdims=True)
    rest = jnp.where(lane == i1, -jnp.inf, logits)
    v2 = jnp.max(rest, axis=1, keepdims=True)
    i2 = jnp.min(jnp.where(rest == v2, lane, LANES), axis=1, keepdims=True)
    e2 = jnp.exp(v2 - v1)
    g1 = 1.0 / (1.0 + e2)
    g2 = e2 * g1
    chosen = jnp.where(lane == i1, 1.0, jnp.where(lane == i2, 1.0, 0.0))
    r = lax.broadcasted_iota(jnp.int32, (tm, tm), 0)
    c = lax.broadcasted_iota(jnp.int32, (tm, tm), 1)
    earlier = jnp.where(r > c, 1.0, 0.0).astype(BF16)
    ranks = _dot(earlier, chosen.astype(BF16)) + carry_s[...]
    tile_cnt = jnp.sum(chosen, axis=0, keepdims=True)
    carry_s[...] = carry_s[...] + tile_cnt
    cnt_o[0] = tile_cnt
    rank1 = jnp.sum(jnp.where(lane == i1, ranks, 0.0), axis=1, keepdims=True)
    rank2 = jnp.sum(jnp.where(lane == i2, ranks, 0.0), axis=1, keepdims=True)
    fields = (i1.astype(F32), i2.astype(F32), rank1, rank2, g1, g2)
    cols = jnp.zeros(logits.shape, F32)
    for k, v in enumerate(fields):
        cols = jnp.where(lane == k, v, cols)
    cols_o[...] = cols
    rows_o[0] = cols.T[:8]


def _sorted_rows(expert, rank, off_ref):
    off = jnp.zeros_like(rank)
    for e in range(N_EXPERTS):
        off = jnp.where(expert == e, off_ref[e].astype(F32), off)
    return rank + off


def _route(x, g, w, b):
    t = x.shape[0]
    tm = MOE_CHUNK
    n_c = t // tm
    return pl.pallas_call(
        _route_kernel,
        grid=(n_c,),
        in_specs=[pl.BlockSpec((tm, D_MODEL), lambda i: (i, 0)),
                  pl.BlockSpec((1, D_MODEL), lambda i: (0, 0)),
                  pl.BlockSpec((D_MODEL, LANES), lambda i: (0, 0)),
                  pl.BlockSpec((1, LANES), lambda i: (0, 0))],
        out_specs=[pl.BlockSpec((tm, D_MODEL), lambda i: (i, 0)),
                   pl.BlockSpec((tm, LANES), lambda i: (i, 0)),
                   pl.BlockSpec((1, 8, tm), lambda i: (i, 0, 0)),
                   pl.BlockSpec((1, 1, LANES), lambda i: (i, 0, 0))],
        out_shape=[jax.ShapeDtypeStruct((t, D_MODEL), BF16), jax.ShapeDtypeStruct((t, LANES), F32),
                   jax.ShapeDtypeStruct((n_c, 8, tm), F32), jax.ShapeDtypeStruct((n_c, 1, LANES), F32)],
        scratch_shapes=[pltpu.VMEM((1, LANES), F32)],
        compiler_params=_cp(1),
        name="moe_route",
    )(x, g, w, b)


def _moe_tile_kernel(se_ref, valid_ref, clo_ref, chi_ref, off_ref,
                     xn_hbm, rows_ref, wg, wu, wd, o_ref, xs_s, xg_s, gate_s, acc_s, cbuf, sem):
    j = pl.program_id(0)
    f = pl.program_id(1)
    n_f = pl.num_programs(1)
    valid = valid_ref[j] == 1

    def chunk_copy(c, slot):
        start = pl.multiple_of(c * MOE_CHUNK, MOE_CHUNK)
        return pltpu.make_async_copy(xn_hbm.at[pl.ds(start, MOE_CHUNK), :], cbuf.at[slot], sem.at[slot])

    @pl.when(valid & (f == 0))
    def _():
        c_lo = clo_ref[j]
        c_hi = chi_ref[j]
        target = (j * MOE_TILE + lax.broadcasted_iota(jnp.int32, (MOE_TILE, 1), 0)).astype(F32)
        xg_s[...] = jnp.zeros_like(xg_s)
        gate_s[...] = jnp.zeros_like(gate_s)
        chunk_copy(c_lo, 0).start()

        def body(c, carry):
            slot = (c - c_lo) & 1
            chunk_copy(c, slot).wait()

            @pl.when(c < c_hi)
            def _():
                chunk_copy(c + 1, 1 - slot).start()

            info = rows_ref[c]
            hit1 = _sorted_rows(info[0:1, :], info[2:3, :], off_ref) == target
            hit2 = _sorted_rows(info[1:2, :], info[3:4, :], off_ref) == target
            pick = jnp.where(hit1, 1.0, jnp.where(hit2, 1.0, 0.0)).astype(BF16)
            xg_s[...] += _dot(pick, cbuf[slot])
            gate_s[...] += jnp.sum(jnp.where(hit1, info[4:5, :], jnp.where(hit2, info[5:6, :], 0.0)),
                                   axis=1, keepdims=True)
            return carry

        lax.fori_loop(c_lo, c_hi + 1, body, 0)
        xs_s[...] = xg_s[...].astype(BF16)
        acc_s[...] = jnp.zeros_like(acc_s)

    @pl.when(valid)
    def _():
        xs = xs_s[...]
        a = _dot(xs, wg[0])
        b = _dot(xs, wu[0])
        h = (a * jax.nn.sigmoid(a) * b).astype(BF16)
        acc_s[...] += _dot(h, wd[0])

    @pl.when(valid & (f == n_f - 1))
    def _():
        o_ref[...] = (acc_s[...] * gate_s[...]).astype(o_ref.dtype)

    @pl.when(jnp.logical_not(valid) & (f == n_f - 1))
    def _():
        o_ref[...] = jnp.zeros_like(o_ref)


def _moe_tiles(sched, xn, rows, wg, wu, wd, tf):
    n_sched = sched[0].shape[0]
    n_e, _, d_ff = wg.shape
    n_f = d_ff // tf
    w_f = lambda f, ok: f * ok + (n_f - 1) * (1 - ok)
    w_in = pl.BlockSpec((1, D_MODEL, tf), lambda j, f, se, ok, *_: (se[j], 0, w_f(f, ok[j])))
    w_out = pl.BlockSpec((1, tf, D_MODEL), lambda j, f, se, ok, *_: (se[j], w_f(f, ok[j]), 0))
    n_rows = n_sched * MOE_TILE
    return pl.pallas_call(
        _moe_tile_kernel,
        grid_spec=pltpu.PrefetchScalarGridSpec(
            num_scalar_prefetch=5, grid=(n_sched, d_ff // tf),
            in_specs=[pl.BlockSpec(memory_space=pl.ANY),
                      pl.BlockSpec(rows.shape, lambda j, f, *_: (0, 0, 0)),
                      w_in, w_in, w_out],
            out_specs=pl.BlockSpec((MOE_TILE, D_MODEL), lambda j, f, *_: (j, 0)),
            scratch_shapes=[pltpu.VMEM((MOE_TILE, D_MODEL), BF16), pltpu.VMEM((MOE_TILE, D_MODEL), F32),
                            pltpu.VMEM((MOE_TILE, 1), F32), pltpu.VMEM((MOE_TILE, D_MODEL), F32),
                            pltpu.VMEM((2, MOE_CHUNK, D_MODEL), BF16), pltpu.SemaphoreType.DMA((2,))]),
        out_shape=jax.ShapeDtypeStruct((n_rows, D_MODEL), BF16),
        compiler_params=_cp(2, 60 * 1024 * 1024),
        name="moe_expert_tiles",
    )(*sched, xn, rows, wg, wu, wd)


def _moe_combine_kernel(start_ref, cnt_ref, limit_ref, off_ref, y_ref, cols_ref, yo_hbm, o_ref, acc_s, wbuf, sem):
    c = pl.program_id(0)
    cols = cols_ref[...]
    row1 = jnp.broadcast_to(_sorted_rows(cols[:, 0:1], cols[:, 2:3], off_ref), (MOE_CHUNK, MOE_WIN))
    row2 = jnp.broadcast_to(_sorted_rows(cols[:, 1:2], cols[:, 3:4], off_ref), (MOE_CHUNK, MOE_WIN))
    lane = lax.broadcasted_iota(jnp.int32, (1, MOE_WIN), 1).astype(F32)
    acc_s[...] = y_ref[...]

    def window(e, k):
        start = start_ref[c * N_EXPERTS + e]
        lo = ((start >> 4) << 4) + k * MOE_WIN
        begin = pl.multiple_of(jnp.minimum(lo, limit_ref[e] - MOE_WIN), MOE_ALIGN)
        copy = pltpu.make_async_copy(yo_hbm.at[pl.ds(begin, MOE_WIN), :], wbuf.at[e], sem.at[e])
        return copy, lo, begin

    def accumulate(e, lo, begin):
        pos = begin.astype(F32) + lane
        fresh = pos >= lo.astype(F32)
        pick = jnp.where(fresh, jnp.where(row1 == pos, 1.0, jnp.where(row2 == pos, 1.0, 0.0)), 0.0)
        acc_s[...] += _dot(pick.astype(BF16), wbuf[e])

    for e in range(N_EXPERTS):
        @pl.when(cnt_ref[c * N_EXPERTS + e] > 0)
        def _():
            window(e, 0)[0].start()

    for e in range(N_EXPERTS):
        cnt = cnt_ref[c * N_EXPERTS + e]

        @pl.when(cnt > 0)
        def _():
            copy, lo, begin = window(e, 0)
            copy.wait()
            accumulate(e, lo, begin)
            start = start_ref[c * N_EXPERTS + e]
            n_win = (start + cnt - ((start >> 4) << 4) + MOE_WIN - 1) // MOE_WIN

            def more(k, carry):
                copy, lo, begin = window(e, k)
                copy.start()
                copy.wait()
                accumulate(e, lo, begin)
                return carry

            lax.fori_loop(1, n_win, more, 0)

    o_ref[...] = acc_s[...]


def _moe_combine(tables, y, cols, yo):
    t = y.shape[0]
    return pl.pallas_call(
        _moe_combine_kernel,
        grid_spec=pltpu.PrefetchScalarGridSpec(
            num_scalar_prefetch=4, grid=(t // MOE_CHUNK,),
            in_specs=[pl.BlockSpec((MOE_CHUNK, D_MODEL), lambda c, *_: (c, 0)),
                      pl.BlockSpec((MOE_CHUNK, LANES), lambda c, *_: (c, 0)),
                      pl.BlockSpec(memory_space=pl.ANY)],
            out_specs=pl.BlockSpec((MOE_CHUNK, D_MODEL), lambda c, *_: (c, 0)),
            scratch_shapes=[pltpu.VMEM((MOE_CHUNK, D_MODEL), F32),
                            pltpu.VMEM((N_EXPERTS, MOE_WIN, D_MODEL), BF16),
                            pltpu.SemaphoreType.DMA((N_EXPERTS,))]),
        out_shape=jax.ShapeDtypeStruct((t, D_MODEL), F32),
        compiler_params=_cp(1),
        name="moe_combine",
    )(*tables, y, cols, yo)


def _moe_top2(y, g, r_w, r_b, wg, wu, wd, tf):
    t = y.shape[0]
    n_chunks = t // MOE_CHUNK
    n_sched = 2 * t // MOE_TILE + N_EXPERTS
    xn, cols, rows, cnt_tiles = _route(y, g, r_w, r_b)

    cnt = cnt_tiles[:, 0, :N_EXPERTS].astype(jnp.int32)
    cum = jnp.concatenate([jnp.zeros((1, N_EXPERTS), jnp.int32), jnp.cumsum(cnt, axis=0)], axis=0)
    total = cum[-1]
    n_tiles = (total + MOE_TILE - 1) // MOE_TILE
    tile_end = jnp.cumsum(n_tiles)
    n_active = tile_end[-1]
    j = jnp.arange(n_sched, dtype=jnp.int32)
    valid = j < n_active
    se = jnp.sum(j[:, None] >= tile_end[None, :], axis=1).astype(jnp.int32)
    last_e = jnp.sum(n_active - 1 >= tile_end).astype(jnp.int32)
    se = jnp.where(valid, se, last_e)
    sm = j - (tile_end - n_tiles)[se]
    first = sm * MOE_TILE
    last = jnp.minimum(first + MOE_TILE, total[se]) - 1
    ends = cum[1:, :].T[se]
    c_lo = jnp.sum(ends <= first[:, None], axis=1).astype(jnp.int32)
    c_hi = jnp.sum(ends <= last[:, None], axis=1).astype(jnp.int32)
    c_lo = jnp.where(valid, c_lo, 0)
    c_hi = jnp.where(valid, jnp.minimum(c_hi, n_chunks - 1), 0)
    off = ((tile_end - n_tiles) * MOE_TILE).astype(jnp.int32)
    sched = (se, valid.astype(jnp.int32), c_lo, c_hi, off)
    yo = _moe_tiles(sched, xn, rows, wg, wu, wd, tf)

    starts = (off[None, :] + cum[:-1, :]).reshape(-1).astype(jnp.int32)
    limits = (off + n_tiles * MOE_TILE).astype(jnp.int32)
    return _moe_combine((starts, cnt.reshape(-1), limits, off), y, cols, yo)


def _fox_sample_kernel(pt_ref, q_ref, kn_ref, vn_ref, lfn_ref, *refs, n_pages):
    k_refs = refs[:n_pages]
    v_refs = refs[n_pages:2 * n_pages]
    lf_refs = refs[2 * n_pages:3 * n_pages]
    o_ref = refs[3 * n_pages]
    q = q_ref[0].astype(F32)
    qb = _lane_bcast_cols(q)
    sub = lax.broadcasted_iota(jnp.int32, (FOX_HEADS, FOX_W), 0)
    lane = lax.broadcasted_iota(jnp.int32, (FOX_HEADS, FOX_W), 1)
    sel = (lane >> 6) == sub
    r = lax.broadcasted_iota(jnp.int32, (PAGE_SIZE, PAGE_SIZE), 0)
    c = lax.broadcasted_iota(jnp.int32, (PAGE_SIZE, PAGE_SIZE), 1)
    later = jnp.where(r > c, 1.0, 0.0).astype(BF16)
    lf_new = lfn_ref[0]
    tail = jnp.zeros((FOX_HEADS, 1), F32)
    scores = [None] * n_pages
    for p in reversed(range(n_pages)):
        lf = lf_refs[p][0]
        bias = _dot_split3(lf, later) + tail + lf_new
        tail = tail + jnp.sum(lf, axis=1, keepdims=True)
        scores[p] = _slab_scores(k_refs[p], qb, range(FOX_HEADS)) + bias
    s_new = jnp.sum(jnp.where(sel, q * kn_ref[0], 0.0), axis=1, keepdims=True)
    m = s_new
    for s in scores:
        m = jnp.maximum(m, jnp.max(s, axis=1, keepdims=True))
    p_new = jnp.exp(s_new - m)
    l = p_new
    probs = [jnp.exp(s - m) for s in scores]
    for pr in probs:
        l = l + jnp.sum(pr, axis=1, keepdims=True)
    accs = []
    for h in range(FOX_HEADS):
        acc = jnp.zeros((FOX_HD, PAGE_SIZE), F32)
        for p in range(n_pages):
            acc = acc + v_refs[p][0, h] * probs[p][h:h + 1, :]
        accs.append(acc)
    o_row = jnp.sum(jnp.concatenate(accs, axis=0).T, axis=0, keepdims=True)
    inv_row = jnp.sum(jnp.where(sel, 1.0 / l, 0.0), axis=0, keepdims=True)
    new_row = jnp.sum(jnp.where(sel, p_new, 0.0), axis=0, keepdims=True)
    o_ref[0] = ((o_row + new_row * vn_ref[0]) * inv_row).astype(o_ref.dtype)


def _lane_bcast_cols(row):
    return jnp.broadcast_to(row, (LANES, row.shape[1])).T


def _slab_scores(kt_ref, qb, order):
    rows = []
    for r in order:
        prod = kt_ref[0, r] * qb[r * FOX_HD:(r + 1) * FOX_HD]
        rows.append(jnp.sum(prod, axis=0, keepdims=True))
    return jnp.concatenate(rows, axis=0)


def _fox_sample(page_table, q, k_new, v_new, lf_new, k_pool, v_pool, lf_pool_t):
    n_seq, n_pages = page_table.shape
    vec = lambda w: pl.BlockSpec((1, 1, w), lambda b, pt: (b, 0, 0))
    page = lambda shape, p: pl.BlockSpec((1,) + shape, lambda b, pt, p=p: (pt[b, p],) + (0,) * len(shape))
    in_specs = [vec(FOX_W), vec(FOX_W), vec(FOX_W), pl.BlockSpec((1, FOX_HEADS, 1), lambda b, pt: (b, 0, 0))]
    in_specs += [page((FOX_HEADS, FOX_HD, PAGE_SIZE), p) for p in range(n_pages)]
    in_specs += [page((FOX_HEADS, FOX_HD, PAGE_SIZE), p) for p in range(n_pages)]
    in_specs += [page((FOX_HEADS, PAGE_SIZE), p) for p in range(n_pages)]
    return pl.pallas_call(
        functools.partial(_fox_sample_kernel, n_pages=n_pages),
        grid_spec=pltpu.PrefetchScalarGridSpec(
            num_scalar_prefetch=1, grid=(n_seq,), in_specs=in_specs, out_specs=vec(FOX_W)),
        out_shape=jax.ShapeDtypeStruct((n_seq, 1, FOX_W), BF16),
        compiler_params=_cp(1),
        name="fox_sample_attn",
    )(page_table, q, k_new, v_new, lf_new, *([k_pool] * n_pages), *([v_pool] * n_pages),
      *([lf_pool_t] * n_pages))


def _diff_sample_kernel(pt_ref, q_ref, kn_ref, vn_ref, lam_ref, gain_ref, rep_ref, *refs, n_pages):
    k_refs = refs[:n_pages]
    v_refs = refs[n_pages:2 * n_pages]
    o_ref = refs[2 * n_pages]
    n_maps = 2 * DIFF_HEADS
    q = q_ref[0].astype(F32)
    qb = _lane_bcast_cols(q)
    sub = lax.broadcasted_iota(jnp.int32, (n_maps, DIFF_W), 0)
    lane = lax.broadcasted_iota(jnp.int32, (n_maps, DIFF_W), 1)
    sel = (lane >> 6) == ((sub & (DIFF_HEADS - 1)) * 2 + (sub >> 3))
    order = [2 * h + c for c in range(2) for h in range(DIFF_HEADS)]
    scores = [_slab_scores(k_refs[p], qb, order) for p in range(n_pages)]
    s_new = jnp.sum(jnp.where(sel, q * kn_ref[0], 0.0), axis=1, keepdims=True)
    m = s_new
    for s in scores:
        m = jnp.maximum(m, jnp.max(s, axis=1, keepdims=True))
    probs = [jnp.exp(s - m) for s in scores]
    p_new = jnp.exp(s_new - m)
    l = p_new
    for pr in probs:
        l = l + jnp.sum(pr, axis=1, keepdims=True)
    inv_l = 1.0 / l
    lam = _diff_lambda(lam_ref)

    def mix(pr):
        pn = pr * inv_l
        return pn[:DIFF_HEADS] - lam * pn[DIFF_HEADS:]

    sub8 = lax.broadcasted_iota(jnp.int32, (DIFF_HEADS, DIFF_HEADS * PAGE_SIZE), 0)
    lane8 = lax.broadcasted_iota(jnp.int32, (DIFF_HEADS, DIFF_HEADS * PAGE_SIZE), 1)
    own = (lane8 & (DIFF_HEADS - 1)) == sub8
    acc = mix(p_new) * vn_ref[0]
    for p in range(n_pages):
        spread = _dot(mix(probs[p]).astype(BF16), rep_ref[...])
        w_rows = jnp.where(own, spread, 0.0).astype(BF16)
        v_rows = v_refs[p][0].reshape(DIFF_HEADS * PAGE_SIZE, 2 * DIFF_HD).astype(BF16)
        acc = acc + _dot(w_rows, v_rows)
    o = _rms(acc) * gain_ref[...] * (1.0 - LAM_INIT)
    o_ref[0] = o.astype(o_ref.dtype)


def _diff_sample(page_table, q, k_new, v_new, lam_vecs, gain, k_pool, v_pool):
    n_seq, n_pages = page_table.shape
    vec = lambda w: pl.BlockSpec((1, 1, w), lambda b, pt: (b, 0, 0))
    head_rows = pl.BlockSpec((1, DIFF_HEADS, 2 * DIFF_HD), lambda b, pt: (b, 0, 0))
    page = lambda shape, p: pl.BlockSpec((1,) + shape, lambda b, pt, p=p: (pt[b, p], 0, 0, 0))
    i = jnp.arange(DIFF_HEADS * PAGE_SIZE)
    rep = (i[None, :] // DIFF_HEADS == jnp.arange(PAGE_SIZE)[:, None]).astype(BF16)
    in_specs = [vec(DIFF_W), vec(DIFF_W), head_rows,
                pl.BlockSpec(lam_vecs.shape, lambda b, pt: (0, 0)),
                pl.BlockSpec(gain.shape, lambda b, pt: (0, 0)),
                pl.BlockSpec(rep.shape, lambda b, pt: (0, 0))]
    in_specs += [page((2 * DIFF_HEADS, DIFF_HD, PAGE_SIZE), p) for p in range(n_pages)]
    in_specs += [page((PAGE_SIZE, DIFF_HEADS, 2 * DIFF_HD), p) for p in range(n_pages)]
    return pl.pallas_call(
        functools.partial(_diff_sample_kernel, n_pages=n_pages),
        grid_spec=pltpu.PrefetchScalarGridSpec(
            num_scalar_prefetch=1, grid=(n_seq,), in_specs=in_specs, out_specs=head_rows),
        out_shape=jax.ShapeDtypeStruct((n_seq, DIFF_HEADS, 2 * DIFF_HD), BF16),
        compiler_params=_cp(1, 60 * 1024 * 1024),
        name="diff_sample_attn",
    )(page_table, q, k_new, v_new, lam_vecs, gain, rep, *([k_pool] * n_pages), *([v_pool] * n_pages))


def _seg_matrix():
    i = jnp.arange(MXU_DIM)
    return jnp.where((i[:, None] // FOX_HD) == (i[None, :] // FOX_HD), 1.0 / FOX_HD, 0.0).astype(BF16)


def kernel(x_prompt, x_sample, cache_fox_k, cache_fox_v, cache_fox_logf, cache_diff_k, cache_diff_v, page_table,
           even_norm_mix, even_w_in, even_b_in, fox_q_gain, fox_k_gain, sgu_ln_gain, sgu_ln_bias, sgu_w_s, sgu_b_s,
           even_w_out, even_norm_ffn, ffn_w_gate, ffn_w_up, ffn_w_down,
           odd_norm_mix, odd_w_in, diff_q_gain, diff_k_gain, diff_lambda_q1, diff_lambda_k1, diff_lambda_q2,
           diff_lambda_k2, diff_subln_gain, odd_w_out, odd_norm_ffn, router_w, router_b,
           moe_w_gate, moe_w_up, moe_w_down):
    batch, seq, _ = x_prompt.shape
    n_seq = x_sample.shape[0]
    n_phys = cache_fox_k.shape[1]
    seg = _seg_matrix()
    row = lambda a: a.reshape(1, -1).astype(F32)

    w_in, b_in = even_w_in[0], even_b_in[0]
    o_f = 3 * FOX_W
    o_u = o_f + FOX_HEADS
    o_vg = o_u + SGU_W
    pad_f = LANES - FOX_HEADS
    pe = dict(
        g_mix=row(even_norm_mix[0]),
        wq=w_in[:, :FOX_W].astype(BF16), wk=w_in[:, FOX_W:2 * FOX_W].astype(BF16),
        wv=w_in[:, 2 * FOX_W:o_f].astype(BF16),
        wf=jnp.pad(w_in[:, o_f:o_u], ((0, 0), (0, pad_f))).astype(BF16),
        wu=w_in[:, o_u:o_vg].astype(BF16), wvg=w_in[:, o_vg:].astype(BF16),
        bq=row(b_in[:FOX_W]), bk=row(b_in[FOX_W:2 * FOX_W]), bv=row(b_in[2 * FOX_W:o_f]),
        bf=row(jnp.pad(b_in[o_f:o_u], (0, pad_f))), bu=row(b_in[o_u:o_vg]), bvg=row(b_in[o_vg:]),
        qg=row(jnp.tile(fox_q_gain[0], FOX_HEADS)), kg=row(jnp.tile(fox_k_gain[0], FOX_HEADS)),
        lng=row(sgu_ln_gain[0]), lnb=row(sgu_ln_bias[0]), seg=seg)
    w_s = sgu_w_s[0]
    b_s_full = jnp.broadcast_to(sgu_b_s[0][:, :, None], (SGU_GROUPS, CHUNK, SGU_CH)).astype(F32)
    w_s_first = row(jnp.repeat(w_s[:, 0, 0], SGU_CH))
    b_s_first = row(jnp.repeat(sgu_b_s[0][:, 0], SGU_CH))
    w_out1 = even_w_out[0][:FOX_W].astype(BF16)
    w_out2 = even_w_out[0][FOX_W:].astype(BF16)
    g_ffn = row(even_norm_ffn[0])
    ffn_g, ffn_u, ffn_d = ffn_w_gate[0].astype(BF16), ffn_w_up[0].astype(BF16), ffn_w_down[0].astype(BF16)

    wo_in = odd_w_in[0]
    po = dict(
        g_mix=row(odd_norm_mix[0]),
        wq=wo_in[:, :DIFF_W].astype(BF16), wk=wo_in[:, DIFF_W:2 * DIFF_W].astype(BF16),
        wv=wo_in[:, 2 * DIFF_W:].astype(BF16),
        qg=row(jnp.tile(diff_q_gain[0].reshape(-1), DIFF_HEADS)),
        kg=row(jnp.tile(diff_k_gain[0].reshape(-1), DIFF_HEADS)), seg=seg)
    lam_vecs = jnp.stack([diff_lambda_q1[0], diff_lambda_k1[0], diff_lambda_q2[0], diff_lambda_k2[0]]).astype(F32)
    subln = row(diff_subln_gain[0])
    subln_t = row(jnp.tile(diff_subln_gain[0], DIFF_HEADS))
    w_oo = odd_w_out[0].astype(BF16)
    g_moe = row(odd_norm_ffn[0])
    r_w = jnp.pad(router_w[0], ((0, 0), (0, LANES - N_EXPERTS))).astype(F32)
    r_b = row(jnp.pad(router_b[0], (0, LANES - N_EXPERTS)))
    moe_g, moe_u, moe_d = moe_w_gate[0].astype(BF16), moe_w_up[0].astype(BF16), moe_w_down[0].astype(BF16)

    xp = x_prompt.reshape(batch * seq, D_MODEL)
    xs = x_sample.reshape(n_seq, D_MODEL)

    def rows_out(a_t, *feat):
        b, _, s = a_t.shape
        n = len(feat)
        a = a_t.reshape((b,) + feat + (s,)).transpose((0, n + 1) + tuple(range(1, n + 1)))
        return a.reshape((1, b, s) + feat)

    q, fkt_p, fvt_p, lft_p, u, vn, kb, vb = _even_in(xp, pe, batch, seq, sample=False)
    tq = 512
    c5 = _cumsum_rows(lft_p.reshape(batch * FOX_HEADS, seq)).reshape(batch, FOX_HEADS // 2, 2, seq // tq, tq)
    fo = _pair_attn(q, kb, vb, batch, seq, (c5,), fox=True, tq=tq)
    yp = _even_out(xp, fo, u, vn, w_s, b_s_full, w_out1, w_out2, single=False)
    yp = _swiglu(yp, g_ffn, ffn_g, ffn_u, ffn_d, tf=1408)

    qs, fkt_s, fvt_s, lft_s, us, vn_s, fk_s, fv_s, lf_s = _even_in(xs, pe, 1, n_seq, sample=True)
    fos = _fox_sample(
        page_table, qs.reshape(n_seq, 1, FOX_W), fk_s.reshape(n_seq, 1, FOX_W), fv_s.reshape(n_seq, 1, FOX_W),
        lf_s.reshape(n_seq, FOX_HEADS, 1),
        cache_fox_k[0].transpose(0, 2, 3, 1), cache_fox_v[0].transpose(0, 2, 3, 1),
        cache_fox_logf[0].transpose(0, 2, 1))
    ys = _even_out(xs, fos.reshape(n_seq, FOX_W), us, vn_s, w_s_first, b_s_first, w_out1, w_out2, single=True)
    ys = _swiglu(ys, g_ffn, ffn_g, ffn_u, ffn_d, tf=1408)

    q, dkt_p, dv_p, kb, vb = _odd_in(yp, po, batch, seq, sample=False)
    do = _pair_attn(q, kb, vb, batch, seq, (lam_vecs, subln), fox=False, tq=tq)
    yp = _res_proj(yp, do, w_oo)
    yp = _moe_top2(yp, g_moe, r_w, r_b, moe_g, moe_u, moe_d, tf=1792)

    qs, dkt_s, dv_s, dk_s = _odd_in(ys, po, 1, n_seq, sample=True)
    dos = _diff_sample(
        page_table, qs.reshape(n_seq, 1, DIFF_W), dk_s.reshape(n_seq, 1, DIFF_W),
        dv_s.reshape(n_seq, DIFF_HEADS, 2 * DIFF_HD), lam_vecs, subln,
        cache_diff_k[0].transpose(0, 2, 3, 4, 1).reshape(n_phys, 2 * DIFF_HEADS, DIFF_HD, PAGE_SIZE),
        cache_diff_v[0])
    ys = _res_proj(ys, dos.reshape(n_seq, DIFF_W), w_oo)
    xn, comb = _router(ys, g_moe, r_w, r_b)
    ys = _moe(ys, xn, comb, moe_g, moe_u, moe_d, tf=896)

    dec = x_sample.shape[1]
    sample_rows = lambda a_t, *feat: rows_out(a_t, *feat).reshape((1, n_seq, dec) + feat)
    return (yp.reshape(batch, seq, D_MODEL), ys.reshape(n_seq, dec, D_MODEL),
            rows_out(fkt_p, FOX_HEADS, FOX_HD), rows_out(fvt_p, FOX_HEADS, FOX_HD), rows_out(lft_p, FOX_HEADS),
            rows_out(dkt_p, DIFF_HEADS, 2, DIFF_HD), dv_p.reshape(1, batch, seq, DIFF_HEADS, 2 * DIFF_HD),
            sample_rows(fkt_s, FOX_HEADS, FOX_HD), sample_rows(fvt_s, FOX_HEADS, FOX_HD),
            sample_rows(lft_s, FOX_HEADS),
            sample_rows(dkt_s, DIFF_HEADS, 2, DIFF_HD), dv_s.reshape(1, n_seq, dec, DIFF_HEADS, 2 * DIFF_HD),
            vn_s.reshape(1, n_seq, dec, SGU_GROUPS, SGU_CH))
```

```python
import functools
import math

import jax
import jax.numpy as jnp
from jax import lax
from jax.experimental import pallas as pl
from jax.experimental.pallas import tpu as pltpu

F32 = jnp.float32
BF16 = jnp.bfloat16

D_MODEL = 1024
FOX_HEADS = 8
FOX_HD = 64
FOX_W = 512
SGU_GROUPS = 4
SGU_CH = 128
SGU_W = 512
CHUNK = 128
DIFF_HEADS = 8
DIFF_HD = 64
DIFF_W = 1024
N_EXPERTS = 8
PAGE_SIZE = 128
NORM_EPS = 1e-6
FOX_SCALE = FOX_HD ** -0.5
DIFF_SCALE = DIFF_HD ** -0.5
LAM_INIT = 0.8 - 0.6 * math.exp(-0.3 * 1)
LOG2E = math.log2(math.e)

LANES = 128
MXU_DIM = 256
VMEM_LIMIT = 52 * 1024 * 1024


def _cp(n_axes, vmem=VMEM_LIMIT):
    return pltpu.CompilerParams(dimension_semantics=("arbitrary",) * n_axes, vmem_limit_bytes=vmem)


def _dot(a, b):
    return jnp.dot(a, b, preferred_element_type=F32)


def _dot_nt(a, b):
    return lax.dot_general(a, b, (((1,), (1,)), ((), ())), preferred_element_type=F32)


def _dot_split3(x, m_bf16):
    hi = x.astype(BF16)
    r1 = x - hi.astype(F32)
    mid = r1.astype(BF16)
    lo = (r1 - mid.astype(F32)).astype(BF16)
    return _dot(hi, m_bf16) + _dot(mid, m_bf16) + _dot(lo, m_bf16)


def _rms(x):
    return x * lax.rsqrt(jnp.mean(x * x, axis=-1, keepdims=True) + NORM_EPS)


def _gelu(x):
    return 0.5 * x * (1.0 + lax.erf(x * (2.0 ** -0.5)))


def _seg_rms(z, seg_ref, gain):
    outs = []
    for j in range(z.shape[1] // MXU_DIM):
        zj = z[:, j * MXU_DIM:(j + 1) * MXU_DIM]
        ms = _dot((zj * zj).astype(BF16), seg_ref[...])
        outs.append(zj * lax.rsqrt(ms + NORM_EPS))
    return jnp.concatenate(outs, axis=1) * gain


def _row_tile(t):
    return 512 if t % 512 == 0 else t


def _even_in_kernel(x_ref, g_ref, wq, wk, wv, wf, wu, wvg, bq, bk, bv, bf, bu, bvg, qg, kg, lng, lnb, seg,
                    q_o, kt_o, vt_o, lft_o, u_o, vn_o, *extra, sample):
    xn = (_rms(x_ref[...]) * g_ref[...]).astype(BF16)

    def proj(w, b):
        return _dot(xn, w[...]) + b[...]

    q = _seg_rms(proj(wq, bq), seg, qg[...])
    k = _seg_rms(proj(wk, bk), seg, kg[...])
    kt_o[0] = k.T
    v = proj(wv, bv)
    v_t = v.T
    vt_o[0] = v_t
    lf = jax.nn.log_sigmoid(proj(wf, bf))
    lft_o[0] = lf.T[:FOX_HEADS]
    if sample:
        k_o, v_o, lf_o = extra
        q_o[...] = (q * FOX_SCALE).astype(BF16)
        k_o[...] = k
        v_o[...] = v
        lf_o[...] = lf[:, :FOX_HEADS]
    else:
        kb_o, vtb_o = extra
        q_o[0, 0] = (q * (FOX_SCALE * LOG2E)).T.astype(BF16)
        kb_o[...] = k.astype(BF16)
        vtb_o[0, 0] = v_t.astype(BF16)
    u_o[...] = _gelu(proj(wu, bu)).astype(BF16)
    vg = _gelu(proj(wvg, bvg))
    parts = []
    for gi in range(SGU_GROUPS):
        c = vg[:, gi * SGU_CH:(gi + 1) * SGU_CH]
        xc = c - jnp.mean(c, axis=-1, keepdims=True)
        parts.append(xc * lax.rsqrt(jnp.mean(xc * xc, axis=-1, keepdims=True) + NORM_EPS))
    vn = jnp.concatenate(parts, axis=1) * lng[...] + lnb[...]
    vn_o[...] = vn.astype(vn_o.dtype)


def _even_in(x, p, batch, seq, sample):
    t = x.shape[0]
    tm = _row_tile(seq)
    n_s = seq // tm
    row = lambda w: pl.BlockSpec((tm, w), lambda i: (i, 0))
    col = lambda w: pl.BlockSpec((1, w, tm), lambda i: (i // n_s, 0, i % n_s))
    full = lambda a: pl.BlockSpec(a.shape, lambda i: (0,) * a.ndim)
    ws = [p["wq"], p["wk"], p["wv"], p["wf"], p["wu"], p["wvg"], p["bq"], p["bk"], p["bv"], p["bf"], p["bu"],
          p["bvg"], p["qg"], p["kg"], p["lng"], p["lnb"], p["seg"]]
    nat = lambda w, dt: jax.ShapeDtypeStruct((t, w), dt)
    tr = lambda w: jax.ShapeDtypeStruct((batch, w, seq), F32)
    slab = pl.BlockSpec((1, 1, FOX_W, tm), lambda i: (i // n_s, i % n_s, 0, 0))
    slab_shape = jax.ShapeDtypeStruct((batch, n_s, FOX_W, tm), BF16)
    out_specs = [row(FOX_W) if sample else slab, col(FOX_W), col(FOX_W), col(FOX_HEADS), row(SGU_W), row(SGU_W)]
    out_shape = [nat(FOX_W, BF16) if sample else slab_shape, tr(FOX_W), tr(FOX_W), tr(FOX_HEADS),
                 nat(SGU_W, BF16), nat(SGU_W, F32 if sample else BF16)]
    if sample:
        out_specs += [row(FOX_W), row(FOX_W), row(FOX_HEADS)]
        out_shape += [nat(FOX_W, F32), nat(FOX_W, F32), nat(FOX_HEADS, F32)]
    else:
        out_specs += [row(FOX_W), slab]
        out_shape += [nat(FOX_W, BF16), slab_shape]
    return pl.pallas_call(
        functools.partial(_even_in_kernel, sample=sample),
        grid=(t // tm,),
        in_specs=[row(D_MODEL), full(p["g_mix"])] + [full(a) for a in ws],
        out_specs=out_specs,
        out_shape=out_shape,
        compiler_params=_cp(1),
        name="even_in_proj",
    )(x, p["g_mix"], *ws)


def _cumsum_kernel(x_ref, o_ref):
    r = lax.broadcasted_iota(jnp.int32, (LANES, LANES), 0)
    c = lax.broadcasted_iota(jnp.int32, (LANES, LANES), 1)
    upper = jnp.where(r <= c, 1.0, 0.0).astype(BF16)
    carry = jnp.zeros((x_ref.shape[0], 1), F32)
    for j in range(x_ref.shape[1] // LANES):
        blk = x_ref[:, j * LANES:(j + 1) * LANES]
        out = _dot_split3(blk, upper) + carry
        o_ref[:, j * LANES:(j + 1) * LANES] = out
        carry = out[:, LANES - 1:LANES]


def _cumsum_rows(x):
    return pl.pallas_call(
        _cumsum_kernel,
        out_shape=jax.ShapeDtypeStruct(x.shape, F32),
        name="logf_cumsum",
    )(x)


ATTN_COLS = 512


def _pair_attn_kernel(*refs, fox, tq):
    if fox:
        qt_ref, k_ref, vt_ref, c_ref, o_ref, cb_s = refs
    else:
        qt_ref, k_ref, vt_ref, lam_ref, gain_ref, o_ref = refs
    qi = pl.program_id(2)
    n_k = vt_ref.shape[1]
    qt = qt_ref[0, 0].astype(F32)
    rowid = lax.broadcasted_iota(jnp.int32, (LANES, 1), 0)
    n_col = tq // ATTN_COLS
    qt_cols = [qt[:, j * ATTN_COLS:(j + 1) * ATTN_COLS] for j in range(n_col)]
    qt_maps = [[jnp.where(rowid < FOX_HD, c, 0.0).astype(BF16) for c in qt_cols],
               [jnp.where(rowid >= FOX_HD, c, 0.0).astype(BF16) for c in qt_cols]]
    key = lax.broadcasted_iota(jnp.int32, (tq, ATTN_COLS), 0)
    qry = lax.broadcasted_iota(jnp.int32, (tq, ATTN_COLS), 1)

    if fox:
        @pl.when(qi == 0)
        def _():
            for h in range(2):
                for kj in range(n_k):
                    cb_s[h, kj] = _lane_bcast_cols(c_ref[0, 0, h, kj:kj + 1, :] * LOG2E)

    def step(kj, carry, masked):
        start = pl.multiple_of(kj * tq, tq)
        k_t = k_ref[pl.ds(start, tq), :]
        vt_t = vt_ref[0, kj]
        new = []
        for h in range(2):
            v_rows = vt_t[h * FOX_HD:(h + 1) * FOX_HD] if fox else vt_t
            for j in range(n_col):
                m, l, acc = carry[h * n_col + j]
                s = _dot(k_t, qt_maps[h][j])
                if fox:
                    s = s - jnp.tile(cb_s[h, kj], (1, ATTN_COLS // LANES))
                if masked:
                    s = jnp.where(key <= qry + j * ATTN_COLS, s, -jnp.inf)
                m_new = jnp.maximum(m, jnp.max(s, axis=0, keepdims=True))
                alpha = jnp.exp2(m - m_new)
                p = jnp.exp2(s - m_new)
                l = alpha * l + jnp.sum(p, axis=0, keepdims=True)
                acc = alpha * acc + _dot(v_rows, p.astype(BF16))
                new.append((m_new, l, acc))
        return tuple(new)

    d_v = FOX_HD if fox else LANES
    init = tuple((jnp.full((1, ATTN_COLS), -jnp.inf, F32), jnp.zeros((1, ATTN_COLS), F32),
                  jnp.zeros((d_v, ATTN_COLS), F32)) for _ in range(2 * n_col))
    carry = lax.fori_loop(0, qi, lambda kj, c: step(kj, c, False), init)
    carry = step(qi, carry, True)
    o_a, o_b = (jnp.concatenate([acc * (1.0 / l) for _, l, acc in carry[h * n_col:(h + 1) * n_col]], axis=1)
                for h in range(2))
    if fox:
        o_ref[...] = jnp.concatenate([o_a, o_b], axis=0).T.astype(o_ref.dtype)
    else:
        lam = _diff_lambda(lam_ref)
        o = (o_a - lam * o_b).T
        o = _rms(o) * gain_ref[...] * (1.0 - LAM_INIT)
        o_ref[...] = o.astype(o_ref.dtype)


def _diff_lambda(lam_ref):
    a = jnp.sum(lam_ref[0:1, :] * lam_ref[1:2, :], axis=1, keepdims=True)
    b = jnp.sum(lam_ref[2:3, :] * lam_ref[3:4, :], axis=1, keepdims=True)
    return jnp.exp(a) - jnp.exp(b) + LAM_INIT


def _pair_attn(qt, k, vt, extra, fox):
    batch, nq, width, tq = qt.shape
    seq = nq * tq
    n_pairs = width // LANES
    qt_spec = pl.BlockSpec((1, 1, LANES, tq), lambda b, h, i: (b, i, h, 0))
    k_spec = pl.BlockSpec((seq, LANES), lambda b, h, i: (b, h))
    vt_spec = pl.BlockSpec((1, nq, LANES, tq), lambda b, h, i: (b, 0, h, 0))
    o_spec = pl.BlockSpec((tq, LANES), lambda b, h, i: (b * nq + i, h))
    if fox:
        extra_specs = [pl.BlockSpec((1, 1, 2, nq, tq), lambda b, h, i: (b, h, 0, 0, 0))]
        scratch = [pltpu.VMEM((2, nq, tq, LANES), F32)]
    else:
        extra_specs = [pl.BlockSpec(a.shape, lambda b, h, i: (0, 0)) for a in extra]
        scratch = []
    return pl.pallas_call(
        functools.partial(_pair_attn_kernel, fox=fox, tq=tq),
        grid=(batch, n_pairs, nq),
        in_specs=[qt_spec, k_spec, vt_spec] + extra_specs,
        out_specs=o_spec,
        out_shape=jax.ShapeDtypeStruct(k.shape, BF16),
        scratch_shapes=scratch,
        compiler_params=_cp(3),
        name="fox_prompt_attn" if fox else "diff_prompt_attn",
    )(qt, k, vt, *extra)


def _even_out_kernel(x_ref, fo_ref, u_ref, vn_ref, ws_ref, bs_ref, w1, w2, o_ref, *, single):
    tm = x_ref.shape[0]
    u = u_ref[...].astype(F32)
    if single:
        so = u * (vn_ref[...].astype(F32) * ws_ref[...] + bs_ref[...])
    else:
        r = lax.broadcasted_iota(jnp.int32, (CHUNK, CHUNK), 0)
        c = lax.broadcasted_iota(jnp.int32, (CHUNK, CHUNK), 1)
        tril = [jnp.where(r >= c, ws_ref[g], 0.0).astype(BF16) for g in range(SGU_GROUPS)]
        rows = []
        for ch in range(tm // CHUNK):
            rs = slice(ch * CHUNK, (ch + 1) * CHUNK)
            cols = []
            for g in range(SGU_GROUPS):
                cs = slice(g * SGU_CH, (g + 1) * SGU_CH)
                mixed = _dot(tril[g], vn_ref[rs, cs]) + bs_ref[g]
                cols.append(u[rs, cs] * mixed)
            rows.append(jnp.concatenate(cols, axis=1))
        so = jnp.concatenate(rows, axis=0)
    o_ref[...] = x_ref[...] + _dot(fo_ref[...], w1[...]) + _dot(so.astype(BF16), w2[...])


def _even_out(x, fo, u, vn, ws, bs, w1, w2, single):
    t = x.shape[0]
    tm = _row_tile(t)
    row = lambda w: pl.BlockSpec((tm, w), lambda i: (i, 0))
    full = lambda a: pl.BlockSpec(a.shape, lambda i: (0,) * a.ndim)
    return pl.pallas_call(
        functools.partial(_even_out_kernel, single=single),
        grid=(t // tm,),
        in_specs=[row(D_MODEL), row(FOX_W), row(SGU_W), row(SGU_W), full(ws), full(bs), full(w1), full(w2)],
        out_specs=row(D_MODEL),
        out_shape=jax.ShapeDtypeStruct((t, D_MODEL), F32),
        compiler_params=_cp(1),
        name="even_out_proj",
    )(x, fo, u, vn, ws, bs, w1, w2)


def _swiglu_kernel(x_ref, g_ref, wg, wu, wd, o_ref, xn_s, acc_s):
    f = pl.program_id(1)

    @pl.when(f == 0)
    def _():
        xn_s[...] = (_rms(x_ref[...]) * g_ref[...]).astype(BF16)
        acc_s[...] = jnp.zeros_like(acc_s)

    xn = xn_s[...]
    a = _dot(xn, wg[...])
    b = _dot(xn, wu[...])
    h = (a * jax.nn.sigmoid(a) * b).astype(BF16)
    acc_s[...] += _dot(h, wd[...])

    @pl.when(f == pl.num_programs(1) - 1)
    def _():
        o_ref[...] = x_ref[...] + acc_s[...]


def _swiglu(x, g, wg, wu, wd, tf):
    t = x.shape[0]
    tm = _row_tile(t)
    d_ff = wg.shape[1]
    return pl.pallas_call(
        _swiglu_kernel,
        grid=(t // tm, d_ff // tf),
        in_specs=[pl.BlockSpec((tm, D_MODEL), lambda i, f: (i, 0)),
                  pl.BlockSpec((1, D_MODEL), lambda i, f: (0, 0)),
                  pl.BlockSpec((D_MODEL, tf), lambda i, f: (0, f)),
                  pl.BlockSpec((D_MODEL, tf), lambda i, f: (0, f)),
                  pl.BlockSpec((tf, D_MODEL), lambda i, f: (f, 0))],
        out_specs=pl.BlockSpec((tm, D_MODEL), lambda i, f: (i, 0)),
        out_shape=jax.ShapeDtypeStruct((t, D_MODEL), F32),
        scratch_shapes=[pltpu.VMEM((tm, D_MODEL), BF16), pltpu.VMEM((tm, D_MODEL), F32)],
        compiler_params=_cp(2),
        name="dense_swiglu",
    )(x, g, wg, wu, wd)


def _odd_in_kernel(x_ref, g_ref, wq, wk, wv, qg, kg, seg, q_o, kt_o, v_o, *extra, sample):
    xn = (_rms(x_ref[...]) * g_ref[...]).astype(BF16)
    q = _seg_rms(_dot(xn, wq[...]), seg, qg[...])
    k = _seg_rms(_dot(xn, wk[...]), seg, kg[...])
    kt_o[0] = k.T
    v = _dot(xn, wv[...])
    v_o[...] = v
    if sample:
        (k_o,) = extra
        q_o[...] = (q * DIFF_SCALE).astype(BF16)
        k_o[...] = k
    else:
        kb_o, vtb_o = extra
        q_o[0, 0] = (q * (DIFF_SCALE * LOG2E)).T.astype(BF16)
        kb_o[...] = k.astype(BF16)
        vtb_o[0, 0] = v.T.astype(BF16)


def _odd_in(x, p, batch, seq, sample):
    t = x.shape[0]
    tm = _row_tile(seq)
    n_s = seq // tm
    row = pl.BlockSpec((tm, DIFF_W), lambda i: (i, 0))
    col = pl.BlockSpec((1, DIFF_W, tm), lambda i: (i // n_s, 0, i % n_s))
    full = lambda a: pl.BlockSpec(a.shape, lambda i: (0,) * a.ndim)
    ws = [p["g_mix"], p["wq"], p["wk"], p["wv"], p["qg"], p["kg"], p["seg"]]
    nat = lambda dt: jax.ShapeDtypeStruct((t, DIFF_W), dt)
    slab = pl.BlockSpec((1, 1, DIFF_W, tm), lambda i: (i // n_s, i % n_s, 0, 0))
    slab_shape = jax.ShapeDtypeStruct((batch, n_s, DIFF_W, tm), BF16)
    out_specs = [row if sample else slab, col, row]
    out_shape = [nat(BF16) if sample else slab_shape, jax.ShapeDtypeStruct((batch, DIFF_W, seq), F32), nat(F32)]
    if sample:
        out_specs += [row]
        out_shape += [nat(F32)]
    else:
        out_specs += [row, slab]
        out_shape += [nat(BF16), slab_shape]
    return pl.pallas_call(
        functools.partial(_odd_in_kernel, sample=sample),
        grid=(t // tm,),
        in_specs=[row] + [full(a) for a in ws],
        out_specs=out_specs,
        out_shape=out_shape,
        compiler_params=_cp(1),
        name="odd_in_proj",
    )(x, *ws)


def _res_proj_kernel(x_ref, a_ref, w_ref, o_ref):
    o_ref[...] = x_ref[...] + _dot(a_ref[...], w_ref[...])


def _res_proj(x, a, w):
    t = x.shape[0]
    tm = _row_tile(t)
    return pl.pallas_call(
        _res_proj_kernel,
        grid=(t // tm,),
        in_specs=[pl.BlockSpec((tm, D_MODEL), lambda i: (i, 0)),
                  pl.BlockSpec((tm, a.shape[1]), lambda i: (i, 0)),
                  pl.BlockSpec(w.shape, lambda i: (0, 0))],
        out_specs=pl.BlockSpec((tm, D_MODEL), lambda i: (i, 0)),
        out_shape=jax.ShapeDtypeStruct((t, D_MODEL), F32),
        compiler_params=_cp(1),
        name="odd_out_proj",
    )(x, a, w)


def _router_kernel(x_ref, g_ref, w_ref, b_ref, xn_o, comb_o):
    xn = _rms(x_ref[...]) * g_ref[...]
    xn_o[...] = xn.astype(BF16)
    w = w_ref[...]
    w_hi = w.astype(BF16)
    w_lo = (w - w_hi.astype(F32)).astype(BF16)
    x_hi = xn.astype(BF16)
    x_lo = (xn - x_hi.astype(F32)).astype(BF16)
    logits = _dot(x_hi, w_hi) + _dot(x_hi, w_lo) + _dot(x_lo, w_hi) + b_ref[...]
    lane = lax.broadcasted_iota(jnp.int32, logits.shape, 1)
    logits = jnp.where(lane < N_EXPERTS, logits, -jnp.inf)
    v1 = jnp.max(logits, axis=1, keepdims=True)
    i1 = jnp.min(jnp.where(logits == v1, lane, LANES), axis=1, keepdims=True)
    rest = jnp.where(lane == i1, -jnp.inf, logits)
    v2 = jnp.max(rest, axis=1, keepdims=True)
    i2 = jnp.min(jnp.where(rest == v2, lane, LANES), axis=1, keepdims=True)
    e2 = jnp.exp(v2 - v1)
    g1 = 1.0 / (1.0 + e2)
    g2 = e2 * g1
    comb_o[...] = jnp.where(lane == i1, g1, 0.0) + jnp.where(lane == i2, g2, 0.0)


def _router(x, g, w, b):
    t = x.shape[0]
    tm = _row_tile(t)
    return pl.pallas_call(
        _router_kernel,
        grid=(t // tm,),
        in_specs=[pl.BlockSpec((tm, D_MODEL), lambda i: (i, 0)),
                  pl.BlockSpec((1, D_MODEL), lambda i: (0, 0)),
                  pl.BlockSpec((D_MODEL, LANES), lambda i: (0, 0)),
                  pl.BlockSpec((1, LANES), lambda i: (0, 0))],
        out_specs=[pl.BlockSpec((tm, D_MODEL), lambda i: (i, 0)), pl.BlockSpec((tm, LANES), lambda i: (i, 0))],
        out_shape=[jax.ShapeDtypeStruct((t, D_MODEL), BF16), jax.ShapeDtypeStruct((t, LANES), F32)],
        compiler_params=_cp(1),
        name="moe_router",
    )(x, g, w, b)


def _moe_kernel(y_ref, xn_ref, comb_ref, wg, wu, wd, o_ref, acc_s):
    e = pl.program_id(1)
    f = pl.program_id(2)

    @pl.when((e == 0) & (f == 0))
    def _():
        acc_s[...] = jnp.zeros_like(acc_s)

    xn = xn_ref[...]
    a = _dot(xn, wg[0])
    b = _dot(xn, wu[0])
    h = (a * jax.nn.sigmoid(a) * b).astype(BF16)
    comb = comb_ref[...]
    lane = lax.broadcasted_iota(jnp.int32, comb.shape, 1)
    gate = jnp.sum(jnp.where(lane == e, comb, 0.0), axis=1, keepdims=True)
    acc_s[...] += gate * _dot(h, wd[0])

    @pl.when((e == pl.num_programs(1) - 1) & (f == pl.num_programs(2) - 1))
    def _():
        o_ref[...] = y_ref[...] + acc_s[...]


def _moe(y, xn, comb, wg, wu, wd, tf):
    t = y.shape[0]
    tm = _row_tile(t)
    n_e, _, d_ff = wg.shape
    return pl.pallas_call(
        _moe_kernel,
        grid=(t // tm, n_e, d_ff // tf),
        in_specs=[pl.BlockSpec((tm, D_MODEL), lambda i, e, f: (i, 0)),
                  pl.BlockSpec((tm, D_MODEL), lambda i, e, f: (i, 0)),
                  pl.BlockSpec((tm, LANES), lambda i, e, f: (i, 0)),
                  pl.BlockSpec((1, D_MODEL, tf), lambda i, e, f: (e, 0, f)),
                  pl.BlockSpec((1, D_MODEL, tf), lambda i, e, f: (e, 0, f)),
                  pl.BlockSpec((1, tf, D_MODEL), lambda i, e, f: (e, f, 0))],
        out_specs=pl.BlockSpec((tm, D_MODEL), lambda i, e, f: (i, 0)),
        out_shape=jax.ShapeDtypeStruct((t, D_MODEL), F32),
        scratch_shapes=[pltpu.VMEM((tm, D_MODEL), F32)],
        compiler_params=_cp(3),
        name="moe_experts",
    )(y, xn, comb, wg, wu, wd)


MOE_TILE = 512
MOE_CHUNK = 512
MOE_WIN = 256
MOE_ALIGN = 16


def _route_kernel(x_ref, g_ref, w_ref, b_ref, xn_o, cols_o, rows_o, cnt_o, carry_s):
    @pl.when(pl.program_id(0) == 0)
    def _():
        carry_s[...] = jnp.zeros_like(carry_s)

    xn = _rms(x_ref[...]) * g_ref[...]
    xn_o[...] = xn.astype(BF16)
    w = w_ref[...]
    w_hi = w.astype(BF16)
    w_lo = (w - w_hi.astype(F32)).astype(BF16)
    x_hi = xn.astype(BF16)
    x_lo = (xn - x_hi.astype(F32)).astype(BF16)
    logits = _dot(x_hi, w_hi) + _dot(x_hi, w_lo) + _dot(x_lo, w_hi) + b_ref[...]
    tm = logits.shape[0]
    lane = lax.broadcasted_iota(jnp.int32, logits.shape, 1)
    logits = jnp.where(lane < N_EXPERTS, logits, -jnp.inf)
    v1 = jnp.max(logits, axis=1, keepdims=True)
    i1 = jnp.min(jnp.where(logits == v1, lane, LANES), axis=1, keepdims=True)
    rest = jnp.where(lane == i1, -jnp.inf, logits)
    v2 = jnp.max(rest, axis=1, keepdims=True)
    i2 = jnp.min(jnp.where(rest == v2, lane, LANES), axis=1, keepdims=True)
    e2 = jnp.exp(v2 - v1)
    g1 = 1.0 / (1.0 + e2)
    g2 = e2 * g1
    chosen = jnp.where(lane == i1, 1.0, jnp.where(lane == i2, 1.0, 0.0))
    r = lax.broadcasted_iota(jnp.int32, (tm, tm), 0)
    c = lax.broadcasted_iota(jnp.int32, (tm, tm), 1)
    earlier = jnp.where(r > c, 1.0, 0.0).astype(BF16)
    ranks = _dot(earlier, chosen.astype(BF16)) + carry_s[...]
    tile_cnt = jnp.sum(chosen, axis=0, keepdims=True)
    carry_s[...] = carry_s[...] + tile_cnt
    cnt_o[0] = tile_cnt
    rank1 = jnp.sum(jnp.where(lane == i1, ranks, 0.0), axis=1, keepdims=True)
    rank2 = jnp.sum(jnp.where(lane == i2, ranks, 0.0), axis=1, keepdims=True)
    fields = (i1.astype(F32), i2.astype(F32), rank1, rank2, g1, g2)
    cols = jnp.zeros(logits.shape, F32)
    for k, v in enumerate(fields):
        cols = jnp.where(lane == k, v, cols)
    cols_o[...] = cols
    rows_o[0] = cols.T[:8]


def _sorted_rows(expert, rank, off_ref):
    off = jnp.zeros_like(rank)
    for e in range(N_EXPERTS):
        off = jnp.where(expert == e, off_ref[e].astype(F32), off)
    return rank + off


def _route(x, g, w, b):
    t = x.shape[0]
    tm = MOE_CHUNK
    n_c = t // tm
    return pl.pallas_call(
        _route_kernel,
        grid=(n_c,),
        in_specs=[pl.BlockSpec((tm, D_MODEL), lambda i: (i, 0)),
                  pl.BlockSpec((1, D_MODEL), lambda i: (0, 0)),
                  pl.BlockSpec((D_MODEL, LANES), lambda i: (0, 0)),
                  pl.BlockSpec((1, LANES), lambda i: (0, 0))],
        out_specs=[pl.BlockSpec((tm, D_MODEL), lambda i: (i, 0)),
                   pl.BlockSpec((tm, LANES), lambda i: (i, 0)),
                   pl.BlockSpec((1, 8, tm), lambda i: (i, 0, 0)),
                   pl.BlockSpec((1, 1, LANES), lambda i: (i, 0, 0))],
        out_shape=[jax.ShapeDtypeStruct((t, D_MODEL), BF16), jax.ShapeDtypeStruct((t, LANES), F32),
                   jax.ShapeDtypeStruct((n_c, 8, tm), F32), jax.ShapeDtypeStruct((n_c, 1, LANES), F32)],
        scratch_shapes=[pltpu.VMEM((1, LANES), F32)],
        compiler_params=_cp(1),
        name="moe_route",
    )(x, g, w, b)


def _moe_tile_kernel(se_ref, valid_ref, clo_ref, chi_ref, off_ref,
                     xn_hbm, rows_ref, wg, wu, wd, o_ref, xs_s, xg_s, gate_s, acc_s, cbuf, sem):
    j = pl.program_id(0)
    f = pl.program_id(1)
    n_f = pl.num_programs(1)
    valid = valid_ref[j] == 1

    def chunk_copy(c, slot):
        start = pl.multiple_of(c * MOE_CHUNK, MOE_CHUNK)
        return pltpu.make_async_copy(xn_hbm.at[pl.ds(start, MOE_CHUNK), :], cbuf.at[slot], sem.at[slot])

    @pl.when(valid & (f == 0))
    def _():
        c_lo = clo_ref[j]
        c_hi = chi_ref[j]
        target = (j * MOE_TILE + lax.broadcasted_iota(jnp.int32, (MOE_TILE, 1), 0)).astype(F32)
        xg_s[...] = jnp.zeros_like(xg_s)
        gate_s[...] = jnp.zeros_like(gate_s)
        chunk_copy(c_lo, 0).start()

        def body(c, carry):
            slot = (c - c_lo) & 1
            chunk_copy(c, slot).wait()

            @pl.when(c < c_hi)
            def _():
                chunk_copy(c + 1, 1 - slot).start()

            info = rows_ref[c]
            hit1 = _sorted_rows(info[0:1, :], info[2:3, :], off_ref) == target
            hit2 = _sorted_rows(info[1:2, :], info[3:4, :], off_ref) == target
            pick = jnp.where(hit1, 1.0, jnp.where(hit2, 1.0, 0.0)).astype(BF16)
            xg_s[...] += _dot(pick, cbuf[slot])
            gate_s[...] += jnp.sum(jnp.where(hit1, info[4:5, :], jnp.where(hit2, info[5:6, :], 0.0)),
                                   axis=1, keepdims=True)
            return carry

        lax.fori_loop(c_lo, c_hi + 1, body, 0)
        xs_s[...] = xg_s[...].astype(BF16)
        acc_s[...] = jnp.zeros_like(acc_s)

    @pl.when(valid)
    def _():
        xs = xs_s[...]
        a = _dot(xs, wg[0])
        b = _dot(xs, wu[0])
        h = (a * jax.nn.sigmoid(a) * b).astype(BF16)
        acc_s[...] += _dot(h, wd[0])

    @pl.when(valid & (f == n_f - 1))
    def _():
        o_ref[...] = (acc_s[...] * gate_s[...]).astype(o_ref.dtype)

    @pl.when(jnp.logical_not(valid) & (f == n_f - 1))
    def _():
        o_ref[...] = jnp.zeros_like(o_ref)


def _moe_tiles(sched, xn, rows, wg, wu, wd, tf):
    n_sched = sched[0].shape[0]
    n_e, _, d_ff = wg.shape
    n_f = d_ff // tf
    w_f = lambda f, ok: f * ok + (n_f - 1) * (1 - ok)
    w_in = pl.BlockSpec((1, D_MODEL, tf), lambda j, f, se, ok, *_: (se[j], 0, w_f(f, ok[j])))
    w_out = pl.BlockSpec((1, tf, D_MODEL), lambda j, f, se, ok, *_: (se[j], w_f(f, ok[j]), 0))
    n_rows = n_sched * MOE_TILE
    return pl.pallas_call(
        _moe_tile_kernel,
        grid_spec=pltpu.PrefetchScalarGridSpec(
            num_scalar_prefetch=5, grid=(n_sched, d_ff // tf),
            in_specs=[pl.BlockSpec(memory_space=pl.ANY),
                      pl.BlockSpec(rows.shape, lambda j, f, *_: (0, 0, 0)),
                      w_in, w_in, w_out],
            out_specs=pl.BlockSpec((MOE_TILE, D_MODEL), lambda j, f, *_: (j, 0)),
            scratch_shapes=[pltpu.VMEM((MOE_TILE, D_MODEL), BF16), pltpu.VMEM((MOE_TILE, D_MODEL), F32),
                            pltpu.VMEM((MOE_TILE, 1), F32), pltpu.VMEM((MOE_TILE, D_MODEL), F32),
                            pltpu.VMEM((2, MOE_CHUNK, D_MODEL), BF16), pltpu.SemaphoreType.DMA((2,))]),
        out_shape=jax.ShapeDtypeStruct((n_rows, D_MODEL), BF16),
        compiler_params=_cp(2, 60 * 1024 * 1024),
        name="moe_expert_tiles",
    )(*sched, xn, rows, wg, wu, wd)


def _moe_combine_kernel(start_ref, cnt_ref, limit_ref, off_ref, y_ref, cols_ref, yo_hbm, o_ref, acc_s, wbuf, sem):
    c = pl.program_id(0)
    cols = cols_ref[...]
    row1 = jnp.broadcast_to(_sorted_rows(cols[:, 0:1], cols[:, 2:3], off_ref), (MOE_CHUNK, MOE_WIN))
    row2 = jnp.broadcast_to(_sorted_rows(cols[:, 1:2], cols[:, 3:4], off_ref), (MOE_CHUNK, MOE_WIN))
    lane = lax.broadcasted_iota(jnp.int32, (1, MOE_WIN), 1).astype(F32)
    acc_s[...] = y_ref[...]

    def window(e, k):
        start = start_ref[c * N_EXPERTS + e]
        lo = ((start >> 4) << 4) + k * MOE_WIN
        begin = pl.multiple_of(jnp.minimum(lo, limit_ref[e] - MOE_WIN), MOE_ALIGN)
        copy = pltpu.make_async_copy(yo_hbm.at[pl.ds(begin, MOE_WIN), :], wbuf.at[e], sem.at[e])
        return copy, lo, begin

    def accumulate(e, lo, begin):
        pos = begin.astype(F32) + lane
        fresh = pos >= lo.astype(F32)
        pick = jnp.where(fresh, jnp.where(row1 == pos, 1.0, jnp.where(row2 == pos, 1.0, 0.0)), 0.0)
        acc_s[...] += _dot(pick.astype(BF16), wbuf[e])

    for e in range(N_EXPERTS):
        @pl.when(cnt_ref[c * N_EXPERTS + e] > 0)
        def _():
            window(e, 0)[0].start()

    for e in range(N_EXPERTS):
        cnt = cnt_ref[c * N_EXPERTS + e]

        @pl.when(cnt > 0)
        def _():
            copy, lo, begin = window(e, 0)
            copy.wait()
            accumulate(e, lo, begin)
            start = start_ref[c * N_EXPERTS + e]
            n_win = (start + cnt - ((start >> 4) << 4) + MOE_WIN - 1) // MOE_WIN

            def more(k, carry):
                copy, lo, begin = window(e, k)
                copy.start()
                copy.wait()
                accumulate(e, lo, begin)
                return carry

            lax.fori_loop(1, n_win, more, 0)

    o_ref[...] = acc_s[...]


def _moe_combine(tables, y, cols, yo):
    t = y.shape[0]
    return pl.pallas_call(
        _moe_combine_kernel,
        grid_spec=pltpu.PrefetchScalarGridSpec(
            num_scalar_prefetch=4, grid=(t // MOE_CHUNK,),
            in_specs=[pl.BlockSpec((MOE_CHUNK, D_MODEL), lambda c, *_: (c, 0)),
                      pl.BlockSpec((MOE_CHUNK, LANES), lambda c, *_: (c, 0)),
                      pl.BlockSpec(memory_space=pl.ANY)],
            out_specs=pl.BlockSpec((MOE_CHUNK, D_MODEL), lambda c, *_: (c, 0)),
            scratch_shapes=[pltpu.VMEM((MOE_CHUNK, D_MODEL), F32),
                            pltpu.VMEM((N_EXPERTS, MOE_WIN, D_MODEL), BF16),
                            pltpu.SemaphoreType.DMA((N_EXPERTS,))]),
        out_shape=jax.ShapeDtypeStruct((t, D_MODEL), F32),
        compiler_params=_cp(1),
        name="moe_combine",
    )(*tables, y, cols, yo)


def _moe_top2(y, g, r_w, r_b, wg, wu, wd, tf):
    t = y.shape[0]
    n_chunks = t // MOE_CHUNK
    n_sched = 2 * t // MOE_TILE + N_EXPERTS
    xn, cols, rows, cnt_tiles = _route(y, g, r_w, r_b)

    cnt = cnt_tiles[:, 0, :N_EXPERTS].astype(jnp.int32)
    cum = jnp.concatenate([jnp.zeros((1, N_EXPERTS), jnp.int32), jnp.cumsum(cnt, axis=0)], axis=0)
    total = cum[-1]
    n_tiles = (total + MOE_TILE - 1) // MOE_TILE
    tile_end = jnp.cumsum(n_tiles)
    n_active = tile_end[-1]
    j = jnp.arange(n_sched, dtype=jnp.int32)
    valid = j < n_active
    se = jnp.sum(j[:, None] >= tile_end[None, :], axis=1).astype(jnp.int32)
    last_e = jnp.sum(n_active - 1 >= tile_end).astype(jnp.int32)
    se = jnp.where(valid, se, last_e)
    sm = j - (tile_end - n_tiles)[se]
    first = sm * MOE_TILE
    last = jnp.minimum(first + MOE_TILE, total[se]) - 1
    ends = cum[1:, :].T[se]
    c_lo = jnp.sum(ends <= first[:, None], axis=1).astype(jnp.int32)
    c_hi = jnp.sum(ends <= last[:, None], axis=1).astype(jnp.int32)
    c_lo = jnp.where(valid, c_lo, 0)
    c_hi = jnp.where(valid, jnp.minimum(c_hi, n_chunks - 1), 0)
    off = ((tile_end - n_tiles) * MOE_TILE).astype(jnp.int32)
    sched = (se, valid.astype(jnp.int32), c_lo, c_hi, off)
    yo = _moe_tiles(sched, xn, rows, wg, wu, wd, tf)

    starts = (off[None, :] + cum[:-1, :]).reshape(-1).astype(jnp.int32)
    limits = (off + n_tiles * MOE_TILE).astype(jnp.int32)
    return _moe_combine((starts, cnt.reshape(-1), limits, off), y, cols, yo)


def _fox_sample_kernel(pt_ref, q_ref, kn_ref, vn_ref, lfn_ref, *refs, n_pages):
    k_refs = refs[:n_pages]
    v_refs = refs[n_pages:2 * n_pages]
    lf_refs = refs[2 * n_pages:3 * n_pages]
    o_ref = refs[3 * n_pages]
    q = q_ref[0].astype(F32)
    qb = _lane_bcast_cols(q)
    sub = lax.broadcasted_iota(jnp.int32, (FOX_HEADS, FOX_W), 0)
    lane = lax.broadcasted_iota(jnp.int32, (FOX_HEADS, FOX_W), 1)
    sel = (lane >> 6) == sub
    r = lax.broadcasted_iota(jnp.int32, (PAGE_SIZE, PAGE_SIZE), 0)
    c = lax.broadcasted_iota(jnp.int32, (PAGE_SIZE, PAGE_SIZE), 1)
    later = jnp.where(r > c, 1.0, 0.0).astype(BF16)
    lf_new = lfn_ref[0]
    tail = jnp.zeros((FOX_HEADS, 1), F32)
    scores = [None] * n_pages
    for p in reversed(range(n_pages)):
        lf = lf_refs[p][0]
        bias = _dot_split3(lf, later) + tail + lf_new
        tail = tail + jnp.sum(lf, axis=1, keepdims=True)
        scores[p] = _slab_scores(k_refs[p], qb, range(FOX_HEADS)) + bias
    s_new = jnp.sum(jnp.where(sel, q * kn_ref[0], 0.0), axis=1, keepdims=True)
    m = s_new
    for s in scores:
        m = jnp.maximum(m, jnp.max(s, axis=1, keepdims=True))
    p_new = jnp.exp(s_new - m)
    l = p_new
    probs = [jnp.exp(s - m) for s in scores]
    for pr in probs:
        l = l + jnp.sum(pr, axis=1, keepdims=True)
    accs = []
    for h in range(FOX_HEADS):
        acc = jnp.zeros((FOX_HD, PAGE_SIZE), F32)
        for p in range(n_pages):
            acc = acc + v_refs[p][0, h] * probs[p][h:h + 1, :]
        accs.append(acc)
    o_row = jnp.sum(jnp.concatenate(accs, axis=0).T, axis=0, keepdims=True)
    inv_row = jnp.sum(jnp.where(sel, 1.0 / l, 0.0), axis=0, keepdims=True)
    new_row = jnp.sum(jnp.where(sel, p_new, 0.0), axis=0, keepdims=True)
    o_ref[0] = ((o_row + new_row * vn_ref[0]) * inv_row).astype(o_ref.dtype)


def _lane_bcast_cols(row):
    return jnp.broadcast_to(row, (LANES, row.shape[1])).T


def _slab_scores(kt_ref, qb, order):
    rows = []
    for r in order:
        prod = kt_ref[0, r] * qb[r * FOX_HD:(r + 1) * FOX_HD]
        rows.append(jnp.sum(prod, axis=0, keepdims=True))
    return jnp.concatenate(rows, axis=0)


def _fox_sample(page_table, q, k_new, v_new, lf_new, k_pool, v_pool, lf_pool_t):
    n_seq, n_pages = page_table.shape
    vec = lambda w: pl.BlockSpec((1, 1, w), lambda b, pt: (b, 0, 0))
    page = lambda shape, p: pl.BlockSpec((1,) + shape, lambda b, pt, p=p: (pt[b, p],) + (0,) * len(shape))
    in_specs = [vec(FOX_W), vec(FOX_W), vec(FOX_W), pl.BlockSpec((1, FOX_HEADS, 1), lambda b, pt: (b, 0, 0))]
    in_specs += [page((FOX_HEADS, FOX_HD, PAGE_SIZE), p) for p in range(n_pages)]
    in_specs += [page((FOX_HEADS, FOX_HD, PAGE_SIZE), p) for p in range(n_pages)]
    in_specs += [page((FOX_HEADS, PAGE_SIZE), p) for p in range(n_pages)]
    return pl.pallas_call(
        functools.partial(_fox_sample_kernel, n_pages=n_pages),
        grid_spec=pltpu.PrefetchScalarGridSpec(
            num_scalar_prefetch=1, grid=(n_seq,), in_specs=in_specs, out_specs=vec(FOX_W)),
        out_shape=jax.ShapeDtypeStruct((n_seq, 1, FOX_W), BF16),
        compiler_params=_cp(1),
        name="fox_sample_attn",
    )(page_table, q, k_new, v_new, lf_new, *([k_pool] * n_pages), *([v_pool] * n_pages),
      *([lf_pool_t] * n_pages))


def _diff_sample_kernel(pt_ref, q_ref, kn_ref, vn_ref, lam_ref, gain_ref, rep_ref, *refs, n_pages):
    k_refs = refs[:n_pages]
    v_refs = refs[n_pages:2 * n_pages]
    o_ref = refs[2 * n_pages]
    n_maps = 2 * DIFF_HEADS
    q = q_ref[0].astype(F32)
    qb = _lane_bcast_cols(q)
    sub = lax.broadcasted_iota(jnp.int32, (n_maps, DIFF_W), 0)
    lane = lax.broadcasted_iota(jnp.int32, (n_maps, DIFF_W), 1)
    sel = (lane >> 6) == ((sub & (DIFF_HEADS - 1)) * 2 + (sub >> 3))
    order = [2 * h + c for c in range(2) for h in range(DIFF_HEADS)]
    scores = [_slab_scores(k_refs[p], qb, order) for p in range(n_pages)]
    s_new = jnp.sum(jnp.where(sel, q * kn_ref[0], 0.0), axis=1, keepdims=True)
    m = s_new
    for s in scores:
        m = jnp.maximum(m, jnp.max(s, axis=1, keepdims=True))
    probs = [jnp.exp(s - m) for s in scores]
    p_new = jnp.exp(s_new - m)
    l = p_new
    for pr in probs:
        l = l + jnp.sum(pr, axis=1, keepdims=True)
    inv_l = 1.0 / l
    lam = _diff_lambda(lam_ref)

    def mix(pr):
        pn = pr * inv_l
        return pn[:DIFF_HEADS] - lam * pn[DIFF_HEADS:]

    sub8 = lax.broadcasted_iota(jnp.int32, (DIFF_HEADS, DIFF_HEADS * PAGE_SIZE), 0)
    lane8 = lax.broadcasted_iota(jnp.int32, (DIFF_HEADS, DIFF_HEADS * PAGE_SIZE), 1)
    own = (lane8 & (DIFF_HEADS - 1)) == sub8
    acc = mix(p_new) * vn_ref[0]
    for p in range(n_pages):
        spread = _dot(mix(probs[p]).astype(BF16), rep_ref[...])
        w_rows = jnp.where(own, spread, 0.0).astype(BF16)
        v_rows = v_refs[p][0].reshape(DIFF_HEADS * PAGE_SIZE, 2 * DIFF_HD).astype(BF16)
        acc = acc + _dot(w_rows, v_rows)
    o = _rms(acc) * gain_ref[...] * (1.0 - LAM_INIT)
    o_ref[0] = o.astype(o_ref.dtype)


def _diff_sample(page_table, q, k_new, v_new, lam_vecs, gain, k_pool, v_pool):
    n_seq, n_pages = page_table.shape
    vec = lambda w: pl.BlockSpec((1, 1, w), lambda b, pt: (b, 0, 0))
    head_rows = pl.BlockSpec((1, DIFF_HEADS, 2 * DIFF_HD), lambda b, pt: (b, 0, 0))
    page = lambda shape, p: pl.BlockSpec((1,) + shape, lambda b, pt, p=p: (pt[b, p], 0, 0, 0))
    i = jnp.arange(DIFF_HEADS * PAGE_SIZE)
    rep = (i[None, :] // DIFF_HEADS == jnp.arange(PAGE_SIZE)[:, None]).astype(BF16)
    in_specs = [vec(DIFF_W), vec(DIFF_W), head_rows,
                pl.BlockSpec(lam_vecs.shape, lambda b, pt: (0, 0)),
                pl.BlockSpec(gain.shape, lambda b, pt: (0, 0)),
                pl.BlockSpec(rep.shape, lambda b, pt: (0, 0))]
    in_specs += [page((2 * DIFF_HEADS, DIFF_HD, PAGE_SIZE), p) for p in range(n_pages)]
    in_specs += [page((PAGE_SIZE, DIFF_HEADS, 2 * DIFF_HD), p) for p in range(n_pages)]
    return pl.pallas_call(
        functools.partial(_diff_sample_kernel, n_pages=n_pages),
        grid_spec=pltpu.PrefetchScalarGridSpec(
            num_scalar_prefetch=1, grid=(n_seq,), in_specs=in_specs, out_specs=head_rows),
        out_shape=jax.ShapeDtypeStruct((n_seq, DIFF_HEADS, 2 * DIFF_HD), BF16),
        compiler_params=_cp(1, 60 * 1024 * 1024),
        name="diff_sample_attn",
    )(page_table, q, k_new, v_new, lam_vecs, gain, rep, *([k_pool] * n_pages), *([v_pool] * n_pages))


def _seg_matrix():
    i = jnp.arange(MXU_DIM)
    return jnp.where((i[:, None] // FOX_HD) == (i[None, :] // FOX_HD), 1.0 / FOX_HD, 0.0).astype(BF16)


def kernel(x_prompt, x_sample, cache_fox_k, cache_fox_v, cache_fox_logf, cache_diff_k, cache_diff_v, page_table,
           even_norm_mix, even_w_in, even_b_in, fox_q_gain, fox_k_gain, sgu_ln_gain, sgu_ln_bias, sgu_w_s, sgu_b_s,
           even_w_out, even_norm_ffn, ffn_w_gate, ffn_w_up, ffn_w_down,
           odd_norm_mix, odd_w_in, diff_q_gain, diff_k_gain, diff_lambda_q1, diff_lambda_k1, diff_lambda_q2,
           diff_lambda_k2, diff_subln_gain, odd_w_out, odd_norm_ffn, router_w, router_b,
           moe_w_gate, moe_w_up, moe_w_down):
    batch, seq, _ = x_prompt.shape
    n_seq = x_sample.shape[0]
    n_phys = cache_fox_k.shape[1]
    seg = _seg_matrix()
    row = lambda a: a.reshape(1, -1).astype(F32)

    w_in, b_in = even_w_in[0], even_b_in[0]
    o_f = 3 * FOX_W
    o_u = o_f + FOX_HEADS
    o_vg = o_u + SGU_W
    pad_f = LANES - FOX_HEADS
    pe = dict(
        g_mix=row(even_norm_mix[0]),
        wq=w_in[:, :FOX_W].astype(BF16), wk=w_in[:, FOX_W:2 * FOX_W].astype(BF16),
        wv=w_in[:, 2 * FOX_W:o_f].astype(BF16),
        wf=jnp.pad(w_in[:, o_f:o_u], ((0, 0), (0, pad_f))).astype(BF16),
        wu=w_in[:, o_u:o_vg].astype(BF16), wvg=w_in[:, o_vg:].astype(BF16),
        bq=row(b_in[:FOX_W]), bk=row(b_in[FOX_W:2 * FOX_W]), bv=row(b_in[2 * FOX_W:o_f]),
        bf=row(jnp.pad(b_in[o_f:o_u], (0, pad_f))), bu=row(b_in[o_u:o_vg]), bvg=row(b_in[o_vg:]),
        qg=row(jnp.tile(fox_q_gain[0], FOX_HEADS)), kg=row(jnp.tile(fox_k_gain[0], FOX_HEADS)),
        lng=row(sgu_ln_gain[0]), lnb=row(sgu_ln_bias[0]), seg=seg)
    w_s = sgu_w_s[0]
    b_s_full = jnp.broadcast_to(sgu_b_s[0][:, :, None], (SGU_GROUPS, CHUNK, SGU_CH)).astype(F32)
    w_s_first = row(jnp.repeat(w_s[:, 0, 0], SGU_CH))
    b_s_first = row(jnp.repeat(sgu_b_s[0][:, 0], SGU_CH))
    w_out1 = even_w_out[0][:FOX_W].astype(BF16)
    w_out2 = even_w_out[0][FOX_W:].astype(BF16)
    g_ffn = row(even_norm_ffn[0])
    ffn_g, ffn_u, ffn_d = ffn_w_gate[0].astype(BF16), ffn_w_up[0].astype(BF16), ffn_w_down[0].astype(BF16)

    wo_in = odd_w_in[0]
    po = dict(
        g_mix=row(odd_norm_mix[0]),
        wq=wo_in[:, :DIFF_W].astype(BF16), wk=wo_in[:, DIFF_W:2 * DIFF_W].astype(BF16),
        wv=wo_in[:, 2 * DIFF_W:].astype(BF16),
        qg=row(jnp.tile(diff_q_gain[0].reshape(-1), DIFF_HEADS)),
        kg=row(jnp.tile(diff_k_gain[0].reshape(-1), DIFF_HEADS)), seg=seg)
    lam_vecs = jnp.stack([diff_lambda_q1[0], diff_lambda_k1[0], diff_lambda_q2[0], diff_lambda_k2[0]]).astype(F32)
    subln = row(diff_subln_gain[0])
    subln_t = row(jnp.tile(diff_subln_gain[0], DIFF_HEADS))
    w_oo = odd_w_out[0].astype(BF16)
    g_moe = row(odd_norm_ffn[0])
    r_w = jnp.pad(router_w[0], ((0, 0), (0, LANES - N_EXPERTS))).astype(F32)
    r_b = row(jnp.pad(router_b[0], (0, LANES - N_EXPERTS)))
    moe_g, moe_u, moe_d = moe_w_gate[0].astype(BF16), moe_w_up[0].astype(BF16), moe_w_down[0].astype(BF16)

    xp = x_prompt.reshape(batch * seq, D_MODEL)
    xs = x_sample.reshape(n_seq, D_MODEL)

    def rows_out(a_t, *feat):
        b, _, s = a_t.shape
        n = len(feat)
        a = a_t.reshape((b,) + feat + (s,)).transpose((0, n + 1) + tuple(range(1, n + 1)))
        return a.reshape((1, b, s) + feat)

    qt, fkt_p, fvt_p, lft_p, u, vn, kb, vtb = _even_in(xp, pe, batch, seq, sample=False)
    tq = qt.shape[-1]
    c5 = _cumsum_rows(lft_p.reshape(batch * FOX_HEADS, seq)).reshape(batch, FOX_HEADS // 2, 2, seq // tq, tq)
    fo = _pair_attn(qt, kb, vtb, (c5,), fox=True)
    yp = _even_out(xp, fo, u, vn, w_s, b_s_full, w_out1, w_out2, single=False)
    yp = _swiglu(yp, g_ffn, ffn_g, ffn_u, ffn_d, tf=1408)

    qs, fkt_s, fvt_s, lft_s, us, vn_s, fk_s, fv_s, lf_s = _even_in(xs, pe, 1, n_seq, sample=True)
    fos = _fox_sample(
        page_table, qs.reshape(n_seq, 1, FOX_W), fk_s.reshape(n_seq, 1, FOX_W), fv_s.reshape(n_seq, 1, FOX_W),
        lf_s.reshape(n_seq, FOX_HEADS, 1),
        cache_fox_k[0].transpose(0, 2, 3, 1), cache_fox_v[0].transpose(0, 2, 3, 1),
        cache_fox_logf[0].transpose(0, 2, 1))
    ys = _even_out(xs, fos.reshape(n_seq, FOX_W), us, vn_s, w_s_first, b_s_first, w_out1, w_out2, single=True)
    ys = _swiglu(ys, g_ffn, ffn_g, ffn_u, ffn_d, tf=1408)

    qt, dkt_p, dv_p, kb, vtb = _odd_in(yp, po, batch, seq, sample=False)
    do = _pair_attn(qt, kb, vtb, (lam_vecs, subln), fox=False)
    yp = _res_proj(yp, do, w_oo)
    yp = _moe_top2(yp, g_moe, r_w, r_b, moe_g, moe_u, moe_d, tf=1792)

    qs, dkt_s, dv_s, dk_s = _odd_in(ys, po, 1, n_seq, sample=True)
    dos = _diff_sample(
        page_table, qs.reshape(n_seq, 1, DIFF_W), dk_s.reshape(n_seq, 1, DIFF_W),
        dv_s.reshape(n_seq, DIFF_HEADS, 2 * DIFF_HD), lam_vecs, subln,
        cache_diff_k[0].transpose(0, 2, 3, 4, 1).reshape(n_phys, 2 * DIFF_HEADS, DIFF_HD, PAGE_SIZE),
        cache_diff_v[0])
    ys = _res_proj(ys, dos.reshape(n_seq, DIFF_W), w_oo)
    xn, comb = _router(ys, g_moe, r_w, r_b)
    ys = _moe(ys, xn, comb, moe_g, moe_u, moe_d, tf=896)

    dec = x_sample.shape[1]
    sample_rows = lambda a_t, *feat: rows_out(a_t, *feat).reshape((1, n_seq, dec) + feat)
    return (yp.reshape(batch, seq, D_MODEL), ys.reshape(n_seq, dec, D_MODEL),
            rows_out(fkt_p, FOX_HEADS, FOX_HD), rows_out(fvt_p, FOX_HEADS, FOX_HD), rows_out(lft_p, FOX_HEADS),
            rows_out(dkt_p, DIFF_HEADS, 2, DIFF_HD), dv_p.reshape(1, batch, seq, DIFF_HEADS, 2 * DIFF_HD),
            sample_rows(fkt_s, FOX_HEADS, FOX_HD), sample_rows(fvt_s, FOX_HEADS, FOX_HD),
            sample_rows(lft_s, FOX_HEADS),
            sample_rows(dkt_s, DIFF_HEADS, 2, DIFF_HD), dv_s.reshape(1, n_seq, dec, DIFF_HEADS, 2 * DIFF_HD),
            vn_s.reshape(1, n_seq, dec, SGU_GROUPS, SGU_CH))
```

```python
import functools
import math

import jax
import jax.numpy as jnp
from jax import lax
from jax.experimental import pallas as pl
from jax.experimental.pallas import tpu as pltpu

F32 = jnp.float32
BF16 = jnp.bfloat16

D_MODEL = 1024
FOX_HEADS = 8
FOX_HD = 64
FOX_W = 512
SGU_GROUPS = 4
SGU_CH = 128
SGU_W = 512
CHUNK = 128
DIFF_HEADS = 8
DIFF_HD = 64
DIFF_W = 1024
N_EXPERTS = 8
PAGE_SIZE = 128
NORM_EPS = 1e-6
FOX_SCALE = FOX_HD ** -0.5
DIFF_SCALE = DIFF_HD ** -0.5
LAM_INIT = 0.8 - 0.6 * math.exp(-0.3 * 1)
LOG2E = math.log2(math.e)

LANES = 128
MXU_DIM = 256
VMEM_LIMIT = 52 * 1024 * 1024


def _cp(n_axes, vmem=VMEM_LIMIT):
    return pltpu.CompilerParams(dimension_semantics=("arbitrary",) * n_axes, vmem_limit_bytes=vmem)


def _dot(a, b):
    return jnp.dot(a, b, preferred_element_type=F32)


def _dot_nt(a, b):
    return lax.dot_general(a, b, (((1,), (1,)), ((), ())), preferred_element_type=F32)


def _dot_split3(x, m_bf16):
    hi = x.astype(BF16)
    r1 = x - hi.astype(F32)
    mid = r1.astype(BF16)
    lo = (r1 - mid.astype(F32)).astype(BF16)
    return _dot(hi, m_bf16) + _dot(mid, m_bf16) + _dot(lo, m_bf16)


def _rms(x):
    return x * lax.rsqrt(jnp.mean(x * x, axis=-1, keepdims=True) + NORM_EPS)


def _gelu(x):
    return 0.5 * x * (1.0 + lax.erf(x * (2.0 ** -0.5)))


def _seg_rms(z, seg_ref, gain):
    outs = []
    for j in range(z.shape[1] // MXU_DIM):
        zj = z[:, j * MXU_DIM:(j + 1) * MXU_DIM]
        ms = _dot((zj * zj).astype(BF16), seg_ref[...])
        outs.append(zj * lax.rsqrt(ms + NORM_EPS))
    return jnp.concatenate(outs, axis=1) * gain


def _row_tile(t):
    return 512 if t % 512 == 0 else t


def _even_in_kernel(x_ref, g_ref, wq, wk, wv, wf, wu, wvg, bq, bk, bv, bf, bu, bvg, qg, kg, lng, lnb, seg,
                    q_o, kt_o, vt_o, lft_o, u_o, vn_o, *extra, sample):
    xn = (_rms(x_ref[...]) * g_ref[...]).astype(BF16)

    def proj(w, b):
        return _dot(xn, w[...]) + b[...]

    q = _seg_rms(proj(wq, bq), seg, qg[...])
    k = _seg_rms(proj(wk, bk), seg, kg[...])
    kt_o[0] = k.T
    v = proj(wv, bv)
    v_t = v.T
    vt_o[0] = v_t
    lf = jax.nn.log_sigmoid(proj(wf, bf))
    lft_o[0] = lf.T[:FOX_HEADS]
    if sample:
        k_o, v_o, lf_o = extra
        q_o[...] = (q * FOX_SCALE).astype(BF16)
        k_o[...] = k
        v_o[...] = v
        lf_o[...] = lf[:, :FOX_HEADS]
    else:
        kb_o, vtb_o = extra
        q_o[0, 0] = (q * (FOX_SCALE * LOG2E)).T.astype(BF16)
        kb_o[...] = k.astype(BF16)
        vtb_o[0, 0] = v_t.astype(BF16)
    u_o[...] = _gelu(proj(wu, bu)).astype(BF16)
    vg = _gelu(proj(wvg, bvg))
    parts = []
    for gi in range(SGU_GROUPS):
        c = vg[:, gi * SGU_CH:(gi + 1) * SGU_CH]
        xc = c - jnp.mean(c, axis=-1, keepdims=True)
        parts.append(xc * lax.rsqrt(jnp.mean(xc * xc, axis=-1, keepdims=True) + NORM_EPS))
    vn = jnp.concatenate(parts, axis=1) * lng[...] + lnb[...]
    vn_o[...] = vn.astype(vn_o.dtype)


def _even_in(x, p, batch, seq, sample):
    t = x.shape[0]
    tm = _row_tile(seq)
    n_s = seq // tm
    row = lambda w: pl.BlockSpec((tm, w), lambda i: (i, 0))
    col = lambda w: pl.BlockSpec((1, w, tm), lambda i: (i // n_s, 0, i % n_s))
    full = lambda a: pl.BlockSpec(a.shape, lambda i: (0,) * a.ndim)
    ws = [p["wq"], p["wk"], p["wv"], p["wf"], p["wu"], p["wvg"], p["bq"], p["bk"], p["bv"], p["bf"], p["bu"],
          p["bvg"], p["qg"], p["kg"], p["lng"], p["lnb"], p["seg"]]
    nat = lambda w, dt: jax.ShapeDtypeStruct((t, w), dt)
    tr = lambda w: jax.ShapeDtypeStruct((batch, w, seq), F32)
    slab = pl.BlockSpec((1, 1, FOX_W, tm), lambda i: (i // n_s, i % n_s, 0, 0))
    slab_shape = jax.ShapeDtypeStruct((batch, n_s, FOX_W, tm), BF16)
    out_specs = [row(FOX_W) if sample else slab, col(FOX_W), col(FOX_W), col(FOX_HEADS), row(SGU_W), row(SGU_W)]
    out_shape = [nat(FOX_W, BF16) if sample else slab_shape, tr(FOX_W), tr(FOX_W), tr(FOX_HEADS),
                 nat(SGU_W, BF16), nat(SGU_W, F32 if sample else BF16)]
    if sample:
        out_specs += [row(FOX_W), row(FOX_W), row(FOX_HEADS)]
        out_shape += [nat(FOX_W, F32), nat(FOX_W, F32), nat(FOX_HEADS, F32)]
    else:
        out_specs += [row(FOX_W), slab]
        out_shape += [nat(FOX_W, BF16), slab_shape]
    return pl.pallas_call(
        functools.partial(_even_in_kernel, sample=sample),
        grid=(t // tm,),
        in_specs=[row(D_MODEL), full(p["g_mix"])] + [full(a) for a in ws],
        out_specs=out_specs,
        out_shape=out_shape,
        compiler_params=_cp(1),
        name="even_in_proj",
    )(x, p["g_mix"], *ws)


def _cumsum_kernel(x_ref, o_ref):
    r = lax.broadcasted_iota(jnp.int32, (LANES, LANES), 0)
    c = lax.broadcasted_iota(jnp.int32, (LANES, LANES), 1)
    upper = jnp.where(r <= c, 1.0, 0.0).astype(BF16)
    carry = jnp.zeros((x_ref.shape[0], 1), F32)
    for j in range(x_ref.shape[1] // LANES):
        blk = x_ref[:, j * LANES:(j + 1) * LANES]
        out = _dot_split3(blk, upper) + carry
        o_ref[:, j * LANES:(j + 1) * LANES] = out
        carry = out[:, LANES - 1:LANES]


def _cumsum_rows(x):
    return pl.pallas_call(
        _cumsum_kernel,
        out_shape=jax.ShapeDtypeStruct(x.shape, F32),
        name="logf_cumsum",
    )(x)


ATTN_COLS = 512


def _pair_attn_kernel(*refs, fox, tq):
    if fox:
        qt_ref, k_ref, vt_ref, c_ref, o_ref, s_buf, cb_s = refs
    else:
        qt_ref, k_ref, vt_ref, lam_ref, gain_ref, o_ref, s_buf = refs
    qi = pl.program_id(2)
    n_k = vt_ref.shape[1]
    qt = qt_ref[0, 0].astype(F32)
    rowid = lax.broadcasted_iota(jnp.int32, (LANES, 1), 0)
    n_col = tq // ATTN_COLS
    qt_cols = [qt[:, j * ATTN_COLS:(j + 1) * ATTN_COLS] for j in range(n_col)]
    qt_maps = [[jnp.where(rowid < FOX_HD, c, 0.0).astype(BF16) for c in qt_cols],
               [jnp.where(rowid >= FOX_HD, c, 0.0).astype(BF16) for c in qt_cols]]
    key = lax.broadcasted_iota(jnp.int32, (tq, ATTN_COLS), 0)
    qry = lax.broadcasted_iota(jnp.int32, (tq, ATTN_COLS), 1)

    if fox:
        @pl.when(qi == 0)
        def _():
            for h in range(2):
                for kj in range(n_k):
                    cb_s[h, kj] = _lane_bcast_cols(c_ref[0, 0, h, kj:kj + 1, :] * LOG2E)

    def scores(kj, slot):
        start = pl.multiple_of(kj * tq, tq)
        k_t = k_ref[pl.ds(start, tq), :]
        for h in range(2):
            for j in range(n_col):
                s = _dot(k_t, qt_maps[h][j])
                if fox:
                    s = s - jnp.tile(cb_s[h, kj], (1, ATTN_COLS // LANES))
                s_buf[slot, h * n_col + j] = s

    def update(kj, slot, carry, masked):
        vt_t = vt_ref[0, kj]
        new = []
        for h in range(2):
            v_rows = vt_t[h * FOX_HD:(h + 1) * FOX_HD] if fox else vt_t
            for j in range(n_col):
                m, l, acc = carry[h * n_col + j]
                s = s_buf[slot, h * n_col + j]
                if masked:
                    s = jnp.where(key <= qry + j * ATTN_COLS, s, -jnp.inf)
                m_new = jnp.maximum(m, jnp.max(s, axis=0, keepdims=True))
                alpha = jnp.exp2(m - m_new)
                p = jnp.exp2(s - m_new)
                l = alpha * l + jnp.sum(p, axis=0, keepdims=True)
                acc = alpha * acc + _dot(v_rows, p.astype(BF16))
                new.append((m_new, l, acc))
        return tuple(new)

    def two_tiles(t, carry):
        kj = 2 * t
        scores(kj + 1, 1)
        carry = update(kj, 0, carry, False)
        scores(kj + 2, 0)
        return update(kj + 1, 1, carry, False)

    def even_tail(carry):
        return update(qi, 0, carry, True)

    def odd_tail(carry):
        scores(qi, 1)
        carry = update(qi - 1, 0, carry, False)
        return update(qi, 1, carry, True)

    d_v = FOX_HD if fox else LANES
    init = tuple((jnp.full((1, ATTN_COLS), -jnp.inf, F32), jnp.zeros((1, ATTN_COLS), F32),
                  jnp.zeros((d_v, ATTN_COLS), F32)) for _ in range(2 * n_col))
    scores(0, 0)
    carry = lax.fori_loop(0, qi // 2, two_tiles, init)
    carry = lax.cond(qi % 2 == 0, even_tail, odd_tail, carry)
    o_a, o_b = (jnp.concatenate([acc * (1.0 / l) for _, l, acc in carry[h * n_col:(h + 1) * n_col]], axis=1)
                for h in range(2))
    if fox:
        o_ref[...] = jnp.concatenate([o_a, o_b], axis=0).T.astype(o_ref.dtype)
    else:
        lam = _diff_lambda(lam_ref)
        o = (o_a - lam * o_b).T
        o = _rms(o) * gain_ref[...] * (1.0 - LAM_INIT)
        o_ref[...] = o.astype(o_ref.dtype)


def _diff_lambda(lam_ref):
    a = jnp.sum(lam_ref[0:1, :] * lam_ref[1:2, :], axis=1, keepdims=True)
    b = jnp.sum(lam_ref[2:3, :] * lam_ref[3:4, :], axis=1, keepdims=True)
    return jnp.exp(a) - jnp.exp(b) + LAM_INIT


def _pair_attn(qt, k, vt, extra, fox):
    batch, nq, width, tq = qt.shape
    seq = nq * tq
    n_pairs = width // LANES
    qt_spec = pl.BlockSpec((1, 1, LANES, tq), lambda b, h, i: (b, i, h, 0))
    k_spec = pl.BlockSpec((seq, LANES), lambda b, h, i: (b, h))
    vt_spec = pl.BlockSpec((1, nq, LANES, tq), lambda b, h, i: (b, 0, h, 0))
    o_spec = pl.BlockSpec((tq, LANES), lambda b, h, i: (b * nq + i, h))
    scratch = [pltpu.VMEM((2, 2 * (tq // ATTN_COLS), tq, ATTN_COLS), F32)]
    if fox:
        extra_specs = [pl.BlockSpec((1, 1, 2, nq, tq), lambda b, h, i: (b, h, 0, 0, 0))]
        scratch += [pltpu.VMEM((2, nq, tq, LANES), F32)]
    else:
        extra_specs = [pl.BlockSpec(a.shape, lambda b, h, i: (0, 0)) for a in extra]
    return pl.pallas_call(
        functools.partial(_pair_attn_kernel, fox=fox, tq=tq),
        grid=(batch, n_pairs, nq),
        in_specs=[qt_spec, k_spec, vt_spec] + extra_specs,
        out_specs=o_spec,
        out_shape=jax.ShapeDtypeStruct(k.shape, BF16),
        scratch_shapes=scratch,
        compiler_params=_cp(3),
        name="fox_prompt_attn" if fox else "diff_prompt_attn",
    )(qt, k, vt, *extra)


def _even_out_kernel(x_ref, fo_ref, u_ref, vn_ref, ws_ref, bs_ref, w1, w2, o_ref, *, single):
    tm = x_ref.shape[0]
    u = u_ref[...].astype(F32)
    if single:
        so = u * (vn_ref[...].astype(F32) * ws_ref[...] + bs_ref[...])
    else:
        r = lax.broadcasted_iota(jnp.int32, (CHUNK, CHUNK), 0)
        c = lax.broadcasted_iota(jnp.int32, (CHUNK, CHUNK), 1)
        tril = [jnp.where(r >= c, ws_ref[g], 0.0).astype(BF16) for g in range(SGU_GROUPS)]
        rows = []
        for ch in range(tm // CHUNK):
            rs = slice(ch * CHUNK, (ch + 1) * CHUNK)
            cols = []
            for g in range(SGU_GROUPS):
                cs = slice(g * SGU_CH, (g + 1) * SGU_CH)
                mixed = _dot(tril[g], vn_ref[rs, cs]) + bs_ref[g]
                cols.append(u[rs, cs] * mixed)
            rows.append(jnp.concatenate(cols, axis=1))
        so = jnp.concatenate(rows, axis=0)
    o_ref[...] = x_ref[...] + _dot(fo_ref[...], w1[...]) + _dot(so.astype(BF16), w2[...])


def _even_out(x, fo, u, vn, ws, bs, w1, w2, single):
    t = x.shape[0]
    tm = _row_tile(t)
    row = lambda w: pl.BlockSpec((tm, w), lambda i: (i, 0))
    full = lambda a: pl.BlockSpec(a.shape, lambda i: (0,) * a.ndim)
    return pl.pallas_call(
        functools.partial(_even_out_kernel, single=single),
        grid=(t // tm,),
        in_specs=[row(D_MODEL), row(FOX_W), row(SGU_W), row(SGU_W), full(ws), full(bs), full(w1), full(w2)],
        out_specs=row(D_MODEL),
        out_shape=jax.ShapeDtypeStruct((t, D_MODEL), F32),
        compiler_params=_cp(1),
        name="even_out_proj",
    )(x, fo, u, vn, ws, bs, w1, w2)


def _swiglu_kernel(x_ref, g_ref, wg, wu, wd, o_ref, xn_s, acc_s):
    f = pl.program_id(1)

    @pl.when(f == 0)
    def _():
        xn_s[...] = (_rms(x_ref[...]) * g_ref[...]).astype(BF16)
        acc_s[...] = jnp.zeros_like(acc_s)

    xn = xn_s[...]
    a = _dot(xn, wg[...])
    b = _dot(xn, wu[...])
    h = (a * jax.nn.sigmoid(a) * b).astype(BF16)
    acc_s[...] += _dot(h, wd[...])

    @pl.when(f == pl.num_programs(1) - 1)
    def _():
        o_ref[...] = x_ref[...] + acc_s[...]


def _swiglu(x, g, wg, wu, wd, tf):
    t = x.shape[0]
    tm = _row_tile(t)
    d_ff = wg.shape[1]
    return pl.pallas_call(
        _swiglu_kernel,
        grid=(t // tm, d_ff // tf),
        in_specs=[pl.BlockSpec((tm, D_MODEL), lambda i, f: (i, 0)),
                  pl.BlockSpec((1, D_MODEL), lambda i, f: (0, 0)),
                  pl.BlockSpec((D_MODEL, tf), lambda i, f: (0, f)),
                  pl.BlockSpec((D_MODEL, tf), lambda i, f: (0, f)),
                  pl.BlockSpec((tf, D_MODEL), lambda i, f: (f, 0))],
        out_specs=pl.BlockSpec((tm, D_MODEL), lambda i, f: (i, 0)),
        out_shape=jax.ShapeDtypeStruct((t, D_MODEL), F32),
        scratch_shapes=[pltpu.VMEM((tm, D_MODEL), BF16), pltpu.VMEM((tm, D_MODEL), F32)],
        compiler_params=_cp(2),
        name="dense_swiglu",
    )(x, g, wg, wu, wd)


def _odd_in_kernel(x_ref, g_ref, wq, wk, wv, qg, kg, seg, q_o, kt_o, v_o, *extra, sample):
    xn = (_rms(x_ref[...]) * g_ref[...]).astype(BF16)
    q = _seg_rms(_dot(xn, wq[...]), seg, qg[...])
    k = _seg_rms(_dot(xn, wk[...]), seg, kg[...])
    kt_o[0] = k.T
    v = _dot(xn, wv[...])
    v_o[...] = v
    if sample:
        (k_o,) = extra
        q_o[...] = (q * DIFF_SCALE).astype(BF16)
        k_o[...] = k
    else:
        kb_o, vtb_o = extra
        q_o[0, 0] = (q * (DIFF_SCALE * LOG2E)).T.astype(BF16)
        kb_o[...] = k.astype(BF16)
        vtb_o[0, 0] = v.T.astype(BF16)


def _odd_in(x, p, batch, seq, sample):
    t = x.shape[0]
    tm = _row_tile(seq)
    n_s = seq // tm
    row = pl.BlockSpec((tm, DIFF_W), lambda i: (i, 0))
    col = pl.BlockSpec((1, DIFF_W, tm), lambda i: (i // n_s, 0, i % n_s))
    full = lambda a: pl.BlockSpec(a.shape, lambda i: (0,) * a.ndim)
    ws = [p["g_mix"], p["wq"], p["wk"], p["wv"], p["qg"], p["kg"], p["seg"]]
    nat = lambda dt: jax.ShapeDtypeStruct((t, DIFF_W), dt)
    slab = pl.BlockSpec((1, 1, DIFF_W, tm), lambda i: (i // n_s, i % n_s, 0, 0))
    slab_shape = jax.ShapeDtypeStruct((batch, n_s, DIFF_W, tm), BF16)
    out_specs = [row if sample else slab, col, row]
    out_shape = [nat(BF16) if sample else slab_shape, jax.ShapeDtypeStruct((batch, DIFF_W, seq), F32), nat(F32)]
    if sample:
        out_specs += [row]
        out_shape += [nat(F32)]
    else:
        out_specs += [row, slab]
        out_shape += [nat(BF16), slab_shape]
    return pl.pallas_call(
        functools.partial(_odd_in_kernel, sample=sample),
        grid=(t // tm,),
        in_specs=[row] + [full(a) for a in ws],
        out_specs=out_specs,
        out_shape=out_shape,
        compiler_params=_cp(1),
        name="odd_in_proj",
    )(x, *ws)


def _res_proj_kernel(x_ref, a_ref, w_ref, o_ref):
    o_ref[...] = x_ref[...] + _dot(a_ref[...], w_ref[...])


def _res_proj(x, a, w):
    t = x.shape[0]
    tm = _row_tile(t)
    return pl.pallas_call(
        _res_proj_kernel,
        grid=(t // tm,),
        in_specs=[pl.BlockSpec((tm, D_MODEL), lambda i: (i, 0)),
                  pl.BlockSpec((tm, a.shape[1]), lambda i: (i, 0)),
                  pl.BlockSpec(w.shape, lambda i: (0, 0))],
        out_specs=pl.BlockSpec((tm, D_MODEL), lambda i: (i, 0)),
        out_shape=jax.ShapeDtypeStruct((t, D_MODEL), F32),
        compiler_params=_cp(1),
        name="odd_out_proj",
    )(x, a, w)


def _router_kernel(x_ref, g_ref, w_ref, b_ref, xn_o, comb_o):
    xn = _rms(x_ref[...]) * g_ref[...]
    xn_o[...] = xn.astype(BF16)
    w = w_ref[...]
    w_hi = w.astype(BF16)
    w_lo = (w - w_hi.astype(F32)).astype(BF16)
    x_hi = xn.astype(BF16)
    x_lo = (xn - x_hi.astype(F32)).astype(BF16)
    logits = _dot(x_hi, w_hi) + _dot(x_hi, w_lo) + _dot(x_lo, w_hi) + b_ref[...]
    lane = lax.broadcasted_iota(jnp.int32, logits.shape, 1)
    logits = jnp.where(lane < N_EXPERTS, logits, -jnp.inf)
    v1 = jnp.max(logits, axis=1, keepdims=True)
    i1 = jnp.min(jnp.where(logits == v1, lane, LANES), axis=1, keepdims=True)
    rest = jnp.where(lane == i1, -jnp.inf, logits)
    v2 = jnp.max(rest, axis=1, keepdims=True)
    i2 = jnp.min(jnp.where(rest == v2, lane, LANES), axis=1, keepdims=True)
    e2 = jnp.exp(v2 - v1)
    g1 = 1.0 / (1.0 + e2)
    g2 = e2 * g1
    comb_o[...] = jnp.where(lane == i1, g1, 0.0) + jnp.where(lane == i2, g2, 0.0)


def _router(x, g, w, b):
    t = x.shape[0]
    tm = _row_tile(t)
    return pl.pallas_call(
        _router_kernel,
        grid=(t // tm,),
        in_specs=[pl.BlockSpec((tm, D_MODEL), lambda i: (i, 0)),
                  pl.BlockSpec((1, D_MODEL), lambda i: (0, 0)),
                  pl.BlockSpec((D_MODEL, LANES), lambda i: (0, 0)),
                  pl.BlockSpec((1, LANES), lambda i: (0, 0))],
        out_specs=[pl.BlockSpec((tm, D_MODEL), lambda i: (i, 0)), pl.BlockSpec((tm, LANES), lambda i: (i, 0))],
        out_shape=[jax.ShapeDtypeStruct((t, D_MODEL), BF16), jax.ShapeDtypeStruct((t, LANES), F32)],
        compiler_params=_cp(1),
        name="moe_router",
    )(x, g, w, b)


def _moe_kernel(y_ref, xn_ref, comb_ref, wg, wu, wd, o_ref, acc_s):
    e = pl.program_id(1)
    f = pl.program_id(2)

    @pl.when((e == 0) & (f == 0))
    def _():
        acc_s[...] = jnp.zeros_like(acc_s)

    xn = xn_ref[...]
    a = _dot(xn, wg[0])
    b = _dot(xn, wu[0])
    h = (a * jax.nn.sigmoid(a) * b).astype(BF16)
    comb = comb_ref[...]
    lane = lax.broadcasted_iota(jnp.int32, comb.shape, 1)
    gate = jnp.sum(jnp.where(lane == e, comb, 0.0), axis=1, keepdims=True)
    acc_s[...] += gate * _dot(h, wd[0])

    @pl.when((e == pl.num_programs(1) - 1) & (f == pl.num_programs(2) - 1))
    def _():
        o_ref[...] = y_ref[...] + acc_s[...]


def _moe(y, xn, comb, wg, wu, wd, tf):
    t = y.shape[0]
    tm = _row_tile(t)
    n_e, _, d_ff = wg.shape
    return pl.pallas_call(
        _moe_kernel,
        grid=(t // tm, n_e, d_ff // tf),
        in_specs=[pl.BlockSpec((tm, D_MODEL), lambda i, e, f: (i, 0)),
                  pl.BlockSpec((tm, D_MODEL), lambda i, e, f: (i, 0)),
                  pl.BlockSpec((tm, LANES), lambda i, e, f: (i, 0)),
                  pl.BlockSpec((1, D_MODEL, tf), lambda i, e, f: (e, 0, f)),
                  pl.BlockSpec((1, D_MODEL, tf), lambda i, e, f: (e, 0, f)),
                  pl.BlockSpec((1, tf, D_MODEL), lambda i, e, f: (e, f, 0))],
        out_specs=pl.BlockSpec((tm, D_MODEL), lambda i, e, f: (i, 0)),
        out_shape=jax.ShapeDtypeStruct((t, D_MODEL), F32),
        scratch_shapes=[pltpu.VMEM((tm, D_MODEL), F32)],
        compiler_params=_cp(3),
        name="moe_experts",
    )(y, xn, comb, wg, wu, wd)


MOE_TILE = 512
MOE_CHUNK = 512
MOE_WIN = 256
MOE_ALIGN = 16


def _route_kernel(x_ref, g_ref, w_ref, b_ref, xn_o, cols_o, rows_o, cnt_o, carry_s):
    @pl.when(pl.program_id(0) == 0)
    def _():
        carry_s[...] = jnp.zeros_like(carry_s)

    xn = _rms(x_ref[...]) * g_ref[...]
    xn_o[...] = xn.astype(BF16)
    w = w_ref[...]
    w_hi = w.astype(BF16)
    w_lo = (w - w_hi.astype(F32)).astype(BF16)
    x_hi = xn.astype(BF16)
    x_lo = (xn - x_hi.astype(F32)).astype(BF16)
    logits = _dot(x_hi, w_hi) + _dot(x_hi, w_lo) + _dot(x_lo, w_hi) + b_ref[...]
    tm = logits.shape[0]
    lane = lax.broadcasted_iota(jnp.int32, logits.shape, 1)
    logits = jnp.where(lane < N_EXPERTS, logits, -jnp.inf)
    v1 = jnp.max(logits, axis=1, keepdims=True)
    i1 = jnp.min(jnp.where(logits == v1, lane, LANES), axis=1, keepdims=True)
    rest = jnp.where(lane == i1, -jnp.inf, logits)
    v2 = jnp.max(rest, axis=1, keepdims=True)
    i2 = jnp.min(jnp.where(rest == v2, lane, LANES), axis=1, keepdims=True)
    e2 = jnp.exp(v2 - v1)
    g1 = 1.0 / (1.0 + e2)
    g2 = e2 * g1
    chosen = jnp.where(lane == i1, 1.0, jnp.where(lane == i2, 1.0, 0.0))
    r = lax.broadcasted_iota(jnp.int32, (tm, tm), 0)
    c = lax.broadcasted_iota(jnp.int32, (tm, tm), 1)
    earlier = jnp.where(r > c, 1.0, 0.0).astype(BF16)
    ranks = _dot(earlier, chosen.astype(BF16)) + carry_s[...]
    tile_cnt = jnp.sum(chosen, axis=0, keepdims=True)
    carry_s[...] = carry_s[...] + tile_cnt
    cnt_o[0] = tile_cnt
    rank1 = jnp.sum(jnp.where(lane == i1, ranks, 0.0), axis=1, keepdims=True)
    rank2 = jnp.sum(jnp.where(lane == i2, ranks, 0.0), axis=1, keepdims=True)
    fields = (i1.astype(F32), i2.astype(F32), rank1, rank2, g1, g2)
    cols = jnp.zeros(logits.shape, F32)
    for k, v in enumerate(fields):
        cols = jnp.where(lane == k, v, cols)
    cols_o[...] = cols
    rows_o[0] = cols.T[:8]


def _sorted_rows(expert, rank, off_ref):
    off = jnp.zeros_like(rank)
    for e in range(N_EXPERTS):
        off = jnp.where(expert == e, off_ref[e].astype(F32), off)
    return rank + off


def _route(x, g, w, b):
    t = x.shape[0]
    tm = MOE_CHUNK
    n_c = t // tm
    return pl.pallas_call(
        _route_kernel,
        grid=(n_c,),
        in_specs=[pl.BlockSpec((tm, D_MODEL), lambda i: (i, 0)),
                  pl.BlockSpec((1, D_MODEL), lambda i: (0, 0)),
                  pl.BlockSpec((D_MODEL, LANES), lambda i: (0, 0)),
                  pl.BlockSpec((1, LANES), lambda i: (0, 0))],
        out_specs=[pl.BlockSpec((tm, D_MODEL), lambda i: (i, 0)),
                   pl.BlockSpec((tm, LANES), lambda i: (i, 0)),
                   pl.BlockSpec((1, 8, tm), lambda i: (i, 0, 0)),
                   pl.BlockSpec((1, 1, LANES), lambda i: (i, 0, 0))],
        out_shape=[jax.ShapeDtypeStruct((t, D_MODEL), BF16), jax.ShapeDtypeStruct((t, LANES), F32),
                   jax.ShapeDtypeStruct((n_c, 8, tm), F32), jax.ShapeDtypeStruct((n_c, 1, LANES), F32)],
        scratch_shapes=[pltpu.VMEM((1, LANES), F32)],
        compiler_params=_cp(1),
        name="moe_route",
    )(x, g, w, b)


def _moe_tile_kernel(se_ref, valid_ref, clo_ref, chi_ref, off_ref,
                     xn_hbm, rows_ref, wg, wu, wd, o_ref, xs_s, xg_s, gate_s, acc_s, cbuf, sem):
    j = pl.program_id(0)
    f = pl.program_id(1)
    n_f = pl.num_programs(1)
    valid = valid_ref[j] == 1

    def chunk_copy(c, slot):
        start = pl.multiple_of(c * MOE_CHUNK, MOE_CHUNK)
        return pltpu.make_async_copy(xn_hbm.at[pl.ds(start, MOE_CHUNK), :], cbuf.at[slot], sem.at[slot])

    @pl.when(valid & (f == 0))
    def _():
        c_lo = clo_ref[j]
        c_hi = chi_ref[j]
        target = (j * MOE_TILE + lax.broadcasted_iota(jnp.int32, (MOE_TILE, 1), 0)).astype(F32)
        xg_s[...] = jnp.zeros_like(xg_s)
        gate_s[...] = jnp.zeros_like(gate_s)
        chunk_copy(c_lo, 0).start()

        def body(c, carry):
            slot = (c - c_lo) & 1
            chunk_copy(c, slot).wait()

            @pl.when(c < c_hi)
            def _():
                chunk_copy(c + 1, 1 - slot).start()

            info = rows_ref[c]
            hit1 = _sorted_rows(info[0:1, :], info[2:3, :], off_ref) == target
            hit2 = _sorted_rows(info[1:2, :], info[3:4, :], off_ref) == target
            pick = jnp.where(hit1, 1.0, jnp.where(hit2, 1.0, 0.0)).astype(BF16)
            xg_s[...] += _dot(pick, cbuf[slot])
            gate_s[...] += jnp.sum(jnp.where(hit1, info[4:5, :], jnp.where(hit2, info[5:6, :], 0.0)),
                                   axis=1, keepdims=True)
            return carry

        lax.fori_loop(c_lo, c_hi + 1, body, 0)
        xs_s[...] = xg_s[...].astype(BF16)
        acc_s[...] = jnp.zeros_like(acc_s)

    @pl.when(valid)
    def _():
        xs = xs_s[...]
        a = _dot(xs, wg[0])
        b = _dot(xs, wu[0])
        h = (a * jax.nn.sigmoid(a) * b).astype(BF16)
        acc_s[...] += _dot(h, wd[0])

    @pl.when(valid & (f == n_f - 1))
    def _():
        o_ref[...] = (acc_s[...] * gate_s[...]).astype(o_ref.dtype)

    @pl.when(jnp.logical_not(valid) & (f == n_f - 1))
    def _():
        o_ref[...] = jnp.zeros_like(o_ref)


def _moe_tiles(sched, xn, rows, wg, wu, wd, tf):
    n_sched = sched[0].shape[0]
    n_e, _, d_ff = wg.shape
    n_f = d_ff // tf
    w_f = lambda f, ok: f * ok + (n_f - 1) * (1 - ok)
    w_in = pl.BlockSpec((1, D_MODEL, tf), lambda j, f, se, ok, *_: (se[j], 0, w_f(f, ok[j])))
    w_out = pl.BlockSpec((1, tf, D_MODEL), lambda j, f, se, ok, *_: (se[j], w_f(f, ok[j]), 0))
    n_rows = n_sched * MOE_TILE
    return pl.pallas_call(
        _moe_tile_kernel,
        grid_spec=pltpu.PrefetchScalarGridSpec(
            num_scalar_prefetch=5, grid=(n_sched, d_ff // tf),
            in_specs=[pl.BlockSpec(memory_space=pl.ANY),
                      pl.BlockSpec(rows.shape, lambda j, f, *_: (0, 0, 0)),
                      w_in, w_in, w_out],
            out_specs=pl.BlockSpec((MOE_TILE, D_MODEL), lambda j, f, *_: (j, 0)),
            scratch_shapes=[pltpu.VMEM((MOE_TILE, D_MODEL), BF16), pltpu.VMEM((MOE_TILE, D_MODEL), F32),
                            pltpu.VMEM((MOE_TILE, 1), F32), pltpu.VMEM((MOE_TILE, D_MODEL), F32),
                            pltpu.VMEM((2, MOE_CHUNK, D_MODEL), BF16), pltpu.SemaphoreType.DMA((2,))]),
        out_shape=jax.ShapeDtypeStruct((n_rows, D_MODEL), BF16),
        compiler_params=_cp(2, 60 * 1024 * 1024),
        name="moe_expert_tiles",
    )(*sched, xn, rows, wg, wu, wd)


def _moe_combine_kernel(start_ref, cnt_ref, limit_ref, off_ref, y_ref, cols_ref, yo_hbm, o_ref, acc_s, wbuf, sem):
    c = pl.program_id(0)
    cols = cols_ref[...]
    row1 = jnp.broadcast_to(_sorted_rows(cols[:, 0:1], cols[:, 2:3], off_ref), (MOE_CHUNK, MOE_WIN))
    row2 = jnp.broadcast_to(_sorted_rows(cols[:, 1:2], cols[:, 3:4], off_ref), (MOE_CHUNK, MOE_WIN))
    lane = lax.broadcasted_iota(jnp.int32, (1, MOE_WIN), 1).astype(F32)
    acc_s[...] = y_ref[...]

    def window(e, k):
        start = start_ref[c * N_EXPERTS + e]
        lo = ((start >> 4) << 4) + k * MOE_WIN
        begin = pl.multiple_of(jnp.minimum(lo, limit_ref[e] - MOE_WIN), MOE_ALIGN)
        copy = pltpu.make_async_copy(yo_hbm.at[pl.ds(begin, MOE_WIN), :], wbuf.at[e], sem.at[e])
        return copy, lo, begin

    def accumulate(e, lo, begin):
        pos = begin.astype(F32) + lane
        fresh = pos >= lo.astype(F32)
        pick = jnp.where(fresh, jnp.where(row1 == pos, 1.0, jnp.where(row2 == pos, 1.0, 0.0)), 0.0)
        acc_s[...] += _dot(pick.astype(BF16), wbuf[e])

    for e in range(N_EXPERTS):
        @pl.when(cnt_ref[c * N_EXPERTS + e] > 0)
        def _():
            window(e, 0)[0].start()

    for e in range(N_EXPERTS):
        cnt = cnt_ref[c * N_EXPERTS + e]

        @pl.when(cnt > 0)
        def _():
            copy, lo, begin = window(e, 0)
            copy.wait()
            accumulate(e, lo, begin)
            start = start_ref[c * N_EXPERTS + e]
            n_win = (start + cnt - ((start >> 4) << 4) + MOE_WIN - 1) // MOE_WIN

            def more(k, carry):
                copy, lo, begin = window(e, k)
                copy.start()
                copy.wait()
                accumulate(e, lo, begin)
                return carry

            lax.fori_loop(1, n_win, more, 0)

    o_ref[...] = acc_s[...]


def _moe_combine(tables, y, cols, yo):
    t = y.shape[0]
    return pl.pallas_call(
        _moe_combine_kernel,
        grid_spec=pltpu.PrefetchScalarGridSpec(
            num_scalar_prefetch=4, grid=(t // MOE_CHUNK,),
            in_specs=[pl.BlockSpec((MOE_CHUNK, D_MODEL), lambda c, *_: (c, 0)),
                      pl.BlockSpec((MOE_CHUNK, LANES), lambda c, *_: (c, 0)),
                      pl.BlockSpec(memory_space=pl.ANY)],
            out_specs=pl.BlockSpec((MOE_CHUNK, D_MODEL), lambda c, *_: (c, 0)),
            scratch_shapes=[pltpu.VMEM((MOE_CHUNK, D_MODEL), F32),
                            pltpu.VMEM((N_EXPERTS, MOE_WIN, D_MODEL), BF16),
                            pltpu.SemaphoreType.DMA((N_EXPERTS,))]),
        out_shape=jax.ShapeDtypeStruct((t, D_MODEL), F32),
        compiler_params=_cp(1),
        name="moe_combine",
    )(*tables, y, cols, yo)


def _moe_top2(y, g, r_w, r_b, wg, wu, wd, tf):
    t = y.shape[0]
    n_chunks = t // MOE_CHUNK
    n_sched = 2 * t // MOE_TILE + N_EXPERTS
    xn, cols, rows, cnt_tiles = _route(y, g, r_w, r_b)

    cnt = cnt_tiles[:, 0, :N_EXPERTS].astype(jnp.int32)
    cum = jnp.concatenate([jnp.zeros((1, N_EXPERTS), jnp.int32), jnp.cumsum(cnt, axis=0)], axis=0)
    total = cum[-1]
    n_tiles = (total + MOE_TILE - 1) // MOE_TILE
    tile_end = jnp.cumsum(n_tiles)
    n_active = tile_end[-1]
    j = jnp.arange(n_sched, dtype=jnp.int32)
    valid = j < n_active
    se = jnp.sum(j[:, None] >= tile_end[None, :], axis=1).astype(jnp.int32)
    last_e = jnp.sum(n_active - 1 >= tile_end).astype(jnp.int32)
    se = jnp.where(valid, se, last_e)
    sm = j - (tile_end - n_tiles)[se]
    first = sm * MOE_TILE
    last = jnp.minimum(first + MOE_TILE, total[se]) - 1
    ends = cum[1:, :].T[se]
    c_lo = jnp.sum(ends <= first[:, None], axis=1).astype(jnp.int32)
    c_hi = jnp.sum(ends <= last[:, None], axis=1).astype(jnp.int32)
    c_lo = jnp.where(valid, c_lo, 0)
    c_hi = jnp.where(valid, jnp.minimum(c_hi, n_chunks - 1), 0)
    off = ((tile_end - n_tiles) * MOE_TILE).astype(jnp.int32)
    sched = (se, valid.astype(jnp.int32), c_lo, c_hi, off)
    yo = _moe_tiles(sched, xn, rows, wg, wu, wd, tf)

    starts = (off[None, :] + cum[:-1, :]).reshape(-1).astype(jnp.int32)
    limits = (off + n_tiles * MOE_TILE).astype(jnp.int32)
    return _moe_combine((starts, cnt.reshape(-1), limits, off), y, cols, yo)


def _fox_sample_kernel(pt_ref, q_ref, kn_ref, vn_ref, lfn_ref, *refs, n_pages):
    k_refs = refs[:n_pages]
    v_refs = refs[n_pages:2 * n_pages]
    lf_refs = refs[2 * n_pages:3 * n_pages]
    o_ref = refs[3 * n_pages]
    q = q_ref[0].astype(F32)
    qb = _lane_bcast_cols(q)
    sub = lax.broadcasted_iota(jnp.int32, (FOX_HEADS, FOX_W), 0)
    lane = lax.broadcasted_iota(jnp.int32, (FOX_HEADS, FOX_W), 1)
    sel = (lane >> 6) == sub
    r = lax.broadcasted_iota(jnp.int32, (PAGE_SIZE, PAGE_SIZE), 0)
    c = lax.broadcasted_iota(jnp.int32, (PAGE_SIZE, PAGE_SIZE), 1)
    later = jnp.where(r > c, 1.0, 0.0).astype(BF16)
    lf_new = lfn_ref[0]
    tail = jnp.zeros((FOX_HEADS, 1), F32)
    scores = [None] * n_pages
    for p in reversed(range(n_pages)):
        lf = lf_refs[p][0]
        bias = _dot_split3(lf, later) + tail + lf_new
        tail = tail + jnp.sum(lf, axis=1, keepdims=True)
        scores[p] = _slab_scores(k_refs[p], qb, range(FOX_HEADS)) + bias
    s_new = jnp.sum(jnp.where(sel, q * kn_ref[0], 0.0), axis=1, keepdims=True)
    m = s_new
    for s in scores:
        m = jnp.maximum(m, jnp.max(s, axis=1, keepdims=True))
    p_new = jnp.exp(s_new - m)
    l = p_new
    probs = [jnp.exp(s - m) for s in scores]
    for pr in probs:
        l = l + jnp.sum(pr, axis=1, keepdims=True)
    accs = []
    for h in range(FOX_HEADS):
        acc = jnp.zeros((FOX_HD, PAGE_SIZE), F32)
        for p in range(n_pages):
            acc = acc + v_refs[p][0, h] * probs[p][h:h + 1, :]
        accs.append(acc)
    o_row = jnp.sum(jnp.concatenate(accs, axis=0).T, axis=0, keepdims=True)
    inv_row = jnp.sum(jnp.where(sel, 1.0 / l, 0.0), axis=0, keepdims=True)
    new_row = jnp.sum(jnp.where(sel, p_new, 0.0), axis=0, keepdims=True)
    o_ref[0] = ((o_row + new_row * vn_ref[0]) * inv_row).astype(o_ref.dtype)


def _lane_bcast_cols(row):
    return jnp.broadcast_to(row, (LANES, row.shape[1])).T


def _slab_scores(kt_ref, qb, order):
    rows = []
    for r in order:
        prod = kt_ref[0, r] * qb[r * FOX_HD:(r + 1) * FOX_HD]
        rows.append(jnp.sum(prod, axis=0, keepdims=True))
    return jnp.concatenate(rows, axis=0)


def _fox_sample(page_table, q, k_new, v_new, lf_new, k_pool, v_pool, lf_pool_t):
    n_seq, n_pages = page_table.shape
    vec = lambda w: pl.BlockSpec((1, 1, w), lambda b, pt: (b, 0, 0))
    page = lambda shape, p: pl.BlockSpec((1,) + shape, lambda b, pt, p=p: (pt[b, p],) + (0,) * len(shape))
    in_specs = [vec(FOX_W), vec(FOX_W), vec(FOX_W), pl.BlockSpec((1, FOX_HEADS, 1), lambda b, pt: (b, 0, 0))]
    in_specs += [page((FOX_HEADS, FOX_HD, PAGE_SIZE), p) for p in range(n_pages)]
    in_specs += [page((FOX_HEADS, FOX_HD, PAGE_SIZE), p) for p in range(n_pages)]
    in_specs += [page((FOX_HEADS, PAGE_SIZE), p) for p in range(n_pages)]
    return pl.pallas_call(
        functools.partial(_fox_sample_kernel, n_pages=n_pages),
        grid_spec=pltpu.PrefetchScalarGridSpec(
            num_scalar_prefetch=1, grid=(n_seq,), in_specs=in_specs, out_specs=vec(FOX_W)),
        out_shape=jax.ShapeDtypeStruct((n_seq, 1, FOX_W), BF16),
        compiler_params=_cp(1),
        name="fox_sample_attn",
    )(page_table, q, k_new, v_new, lf_new, *([k_pool] * n_pages), *([v_pool] * n_pages),
      *([lf_pool_t] * n_pages))


def _diff_sample_kernel(pt_ref, q_ref, kn_ref, vn_ref, lam_ref, gain_ref, rep_ref, *refs, n_pages):
    k_refs = refs[:n_pages]
    v_refs = refs[n_pages:2 * n_pages]
    o_ref = refs[2 * n_pages]
    n_maps = 2 * DIFF_HEADS
    q = q_ref[0].astype(F32)
    qb = _lane_bcast_cols(q)
    sub = lax.broadcasted_iota(jnp.int32, (n_maps, DIFF_W), 0)
    lane = lax.broadcasted_iota(jnp.int32, (n_maps, DIFF_W), 1)
    sel = (lane >> 6) == ((sub & (DIFF_HEADS - 1)) * 2 + (sub >> 3))
    order = [2 * h + c for c in range(2) for h in range(DIFF_HEADS)]
    scores = [_slab_scores(k_refs[p], qb, order) for p in range(n_pages)]
    s_new = jnp.sum(jnp.where(sel, q * kn_ref[0], 0.0), axis=1, keepdims=True)
    m = s_new
    for s in scores:
        m = jnp.maximum(m, jnp.max(s, axis=1, keepdims=True))
    probs = [jnp.exp(s - m) for s in scores]
    p_new = jnp.exp(s_new - m)
    l = p_new
    for pr in probs:
        l = l + jnp.sum(pr, axis=1, keepdims=True)
    inv_l = 1.0 / l
    lam = _diff_lambda(lam_ref)

    def mix(pr):
        pn = pr * inv_l
        return pn[:DIFF_HEADS] - lam * pn[DIFF_HEADS:]

    sub8 = lax.broadcasted_iota(jnp.int32, (DIFF_HEADS, DIFF_HEADS * PAGE_SIZE), 0)
    lane8 = lax.broadcasted_iota(jnp.int32, (DIFF_HEADS, DIFF_HEADS * PAGE_SIZE), 1)
    own = (lane8 & (DIFF_HEADS - 1)) == sub8
    acc = mix(p_new) * vn_ref[0]
    for p in range(n_pages):
        spread = _dot(mix(probs[p]).astype(BF16), rep_ref[...])
        w_rows = jnp.where(own, spread, 0.0).astype(BF16)
        v_rows = v_refs[p][0].reshape(DIFF_HEADS * PAGE_SIZE, 2 * DIFF_HD).astype(BF16)
        acc = acc + _dot(w_rows, v_rows)
    o = _rms(acc) * gain_ref[...] * (1.0 - LAM_INIT)
    o_ref[0] = o.astype(o_ref.dtype)


def _diff_sample(page_table, q, k_new, v_new, lam_vecs, gain, k_pool, v_pool):
    n_seq, n_pages = page_table.shape
    vec = lambda w: pl.BlockSpec((1, 1, w), lambda b, pt: (b, 0, 0))
    head_rows = pl.BlockSpec((1, DIFF_HEADS, 2 * DIFF_HD), lambda b, pt: (b, 0, 0))
    page = lambda shape, p: pl.BlockSpec((1,) + shape, lambda b, pt, p=p: (pt[b, p], 0, 0, 0))
    i = jnp.arange(DIFF_HEADS * PAGE_SIZE)
    rep = (i[None, :] // DIFF_HEADS == jnp.arange(PAGE_SIZE)[:, None]).astype(BF16)
    in_specs = [vec(DIFF_W), vec(DIFF_W), head_rows,
                pl.BlockSpec(lam_vecs.shape, lambda b, pt: (0, 0)),
                pl.BlockSpec(gain.shape, lambda b, pt: (0, 0)),
                pl.BlockSpec(rep.shape, lambda b, pt: (0, 0))]
    in_specs += [page((2 * DIFF_HEADS, DIFF_HD, PAGE_SIZE), p) for p in range(n_pages)]
    in_specs += [page((PAGE_SIZE, DIFF_HEADS, 2 * DIFF_HD), p) for p in range(n_pages)]
    return pl.pallas_call(
        functools.partial(_diff_sample_kernel, n_pages=n_pages),
        grid_spec=pltpu.PrefetchScalarGridSpec(
            num_scalar_prefetch=1, grid=(n_seq,), in_specs=in_specs, out_specs=head_rows),
        out_shape=jax.ShapeDtypeStruct((n_seq, DIFF_HEADS, 2 * DIFF_HD), BF16),
        compiler_params=_cp(1, 60 * 1024 * 1024),
        name="diff_sample_attn",
    )(page_table, q, k_new, v_new, lam_vecs, gain, rep, *([k_pool] * n_pages), *([v_pool] * n_pages))


def _seg_matrix():
    i = jnp.arange(MXU_DIM)
    return jnp.where((i[:, None] // FOX_HD) == (i[None, :] // FOX_HD), 1.0 / FOX_HD, 0.0).astype(BF16)


def kernel(x_prompt, x_sample, cache_fox_k, cache_fox_v, cache_fox_logf, cache_diff_k, cache_diff_v, page_table,
           even_norm_mix, even_w_in, even_b_in, fox_q_gain, fox_k_gain, sgu_ln_gain, sgu_ln_bias, sgu_w_s, sgu_b_s,
           even_w_out, even_norm_ffn, ffn_w_gate, ffn_w_up, ffn_w_down,
           odd_norm_mix, odd_w_in, diff_q_gain, diff_k_gain, diff_lambda_q1, diff_lambda_k1, diff_lambda_q2,
           diff_lambda_k2, diff_subln_gain, odd_w_out, odd_norm_ffn, router_w, router_b,
           moe_w_gate, moe_w_up, moe_w_down):
    batch, seq, _ = x_prompt.shape
    n_seq = x_sample.shape[0]
    n_phys = cache_fox_k.shape[1]
    seg = _seg_matrix()
    row = lambda a: a.reshape(1, -1).astype(F32)

    w_in, b_in = even_w_in[0], even_b_in[0]
    o_f = 3 * FOX_W
    o_u = o_f + FOX_HEADS
    o_vg = o_u + SGU_W
    pad_f = LANES - FOX_HEADS
    pe = dict(
        g_mix=row(even_norm_mix[0]),
        wq=w_in[:, :FOX_W].astype(BF16), wk=w_in[:, FOX_W:2 * FOX_W].astype(BF16),
        wv=w_in[:, 2 * FOX_W:o_f].astype(BF16),
        wf=jnp.pad(w_in[:, o_f:o_u], ((0, 0), (0, pad_f))).astype(BF16),
        wu=w_in[:, o_u:o_vg].astype(BF16), wvg=w_in[:, o_vg:].astype(BF16),
        bq=row(b_in[:FOX_W]), bk=row(b_in[FOX_W:2 * FOX_W]), bv=row(b_in[2 * FOX_W:o_f]),
        bf=row(jnp.pad(b_in[o_f:o_u], (0, pad_f))), bu=row(b_in[o_u:o_vg]), bvg=row(b_in[o_vg:]),
        qg=row(jnp.tile(fox_q_gain[0], FOX_HEADS)), kg=row(jnp.tile(fox_k_gain[0], FOX_HEADS)),
        lng=row(sgu_ln_gain[0]), lnb=row(sgu_ln_bias[0]), seg=seg)
    w_s = sgu_w_s[0]
    b_s_full = jnp.broadcast_to(sgu_b_s[0][:, :, None], (SGU_GROUPS, CHUNK, SGU_CH)).astype(F32)
    w_s_first = row(jnp.repeat(w_s[:, 0, 0], SGU_CH))
    b_s_first = row(jnp.repeat(sgu_b_s[0][:, 0], SGU_CH))
    w_out1 = even_w_out[0][:FOX_W].astype(BF16)
    w_out2 = even_w_out[0][FOX_W:].astype(BF16)
    g_ffn = row(even_norm_ffn[0])
    ffn_g, ffn_u, ffn_d = ffn_w_gate[0].astype(BF16), ffn_w_up[0].astype(BF16), ffn_w_down[0].astype(BF16)

    wo_in = odd_w_in[0]
    po = dict(
        g_mix=row(odd_norm_mix[0]),
        wq=wo_in[:, :DIFF_W].astype(BF16), wk=wo_in[:, DIFF_W:2 * DIFF_W].astype(BF16),
        wv=wo_in[:, 2 * DIFF_W:].astype(BF16),
        qg=row(jnp.tile(diff_q_gain[0].reshape(-1), DIFF_HEADS)),
        kg=row(jnp.tile(diff_k_gain[0].reshape(-1), DIFF_HEADS)), seg=seg)
    lam_vecs = jnp.stack([diff_lambda_q1[0], diff_lambda_k1[0], diff_lambda_q2[0], diff_lambda_k2[0]]).astype(F32)
    subln = row(diff_subln_gain[0])
    subln_t = row(jnp.tile(diff_subln_gain[0], DIFF_HEADS))
    w_oo = odd_w_out[0].astype(BF16)
    g_moe = row(odd_norm_ffn[0])
    r_w = jnp.pad(router_w[0], ((0, 0), (0, LANES - N_EXPERTS))).astype(F32)
    r_b = row(jnp.pad(router_b[0], (0, LANES - N_EXPERTS)))
    moe_g, moe_u, moe_d = moe_w_gate[0].astype(BF16), moe_w_up[0].astype(BF16), moe_w_down[0].astype(BF16)

    xp = x_prompt.reshape(batch * seq, D_MODEL)
    xs = x_sample.reshape(n_seq, D_MODEL)

    def rows_out(a_t, *feat):
        b, _, s = a_t.shape
        n = len(feat)
        a = a_t.reshape((b,) + feat + (s,)).transpose((0, n + 1) + tuple(range(1, n + 1)))
        return a.reshape((1, b, s) + feat)

    qt, fkt_p, fvt_p, lft_p, u, vn, kb, vtb = _even_in(xp, pe, batch, seq, sample=False)
    tq = qt.shape[-1]
    c5 = _cumsum_rows(lft_p.reshape(batch * FOX_HEADS, seq)).reshape(batch, FOX_HEADS // 2, 2, seq // tq, tq)
    fo = _pair_attn(qt, kb, vtb, (c5,), fox=True)
    yp = _even_out(xp, fo, u, vn, w_s, b_s_full, w_out1, w_out2, single=False)
    yp = _swiglu(yp, g_ffn, ffn_g, ffn_u, ffn_d, tf=1408)

    qs, fkt_s, fvt_s, lft_s, us, vn_s, fk_s, fv_s, lf_s = _even_in(xs, pe, 1, n_seq, sample=True)
    fos = _fox_sample(
        page_table, qs.reshape(n_seq, 1, FOX_W), fk_s.reshape(n_seq, 1, FOX_W), fv_s.reshape(n_seq, 1, FOX_W),
        lf_s.reshape(n_seq, FOX_HEADS, 1),
        cache_fox_k[0].transpose(0, 2, 3, 1), cache_fox_v[0].transpose(0, 2, 3, 1),
        cache_fox_logf[0].transpose(0, 2, 1))
    ys = _even_out(xs, fos.reshape(n_seq, FOX_W), us, vn_s, w_s_first, b_s_first, w_out1, w_out2, single=True)
    ys = _swiglu(ys, g_ffn, ffn_g, ffn_u, ffn_d, tf=1408)

    qt, dkt_p, dv_p, kb, vtb = _odd_in(yp, po, batch, seq, sample=False)
    do = _pair_attn(qt, kb, vtb, (lam_vecs, subln), fox=False)
    yp = _res_proj(yp, do, w_oo)
    yp = _moe_top2(yp, g_moe, r_w, r_b, moe_g, moe_u, moe_d, tf=1792)

    qs, dkt_s, dv_s, dk_s = _odd_in(ys, po, 1, n_seq, sample=True)
    dos = _diff_sample(
        page_table, qs.reshape(n_seq, 1, DIFF_W), dk_s.reshape(n_seq, 1, DIFF_W),
        dv_s.reshape(n_seq, DIFF_HEADS, 2 * DIFF_HD), lam_vecs, subln,
        cache_diff_k[0].transpose(0, 2, 3, 4, 1).reshape(n_phys, 2 * DIFF_HEADS, DIFF_HD, PAGE_SIZE),
        cache_diff_v[0])
    ys = _res_proj(ys, dos.reshape(n_seq, DIFF_W), w_oo)
    xn, comb = _router(ys, g_moe, r_w, r_b)
    ys = _moe(ys, xn, comb, moe_g, moe_u, moe_d, tf=896)

    dec = x_sample.shape[1]
    sample_rows = lambda a_t, *feat: rows_out(a_t, *feat).reshape((1, n_seq, dec) + feat)
    return (yp.reshape(batch, seq, D_MODEL), ys.reshape(n_seq, dec, D_MODEL),
            rows_out(fkt_p, FOX_HEADS, FOX_HD), rows_out(fvt_p, FOX_HEADS, FOX_HD), rows_out(lft_p, FOX_HEADS),
            rows_out(dkt_p, DIFF_HEADS, 2, DIFF_HD), dv_p.reshape(1, batch, seq, DIFF_HEADS, 2 * DIFF_HD),
            sample_rows(fkt_s, FOX_HEADS, FOX_HD), sample_rows(fvt_s, FOX_HEADS, FOX_HD),
            sample_rows(lft_s, FOX_HEADS),
            sample_rows(dkt_s, DIFF_HEADS, 2, DIFF_HD), dv_s.reshape(1, n_seq, dec, DIFF_HEADS, 2 * DIFF_HD),
            vn_s.reshape(1, n_seq, dec, SGU_GROUPS, SGU_CH))
```

```python
import functools
import math

import jax
import jax.numpy as jnp
from jax import lax
from jax.experimental import pallas as pl
from jax.experimental.pallas import tpu as pltpu

F32 = jnp.float32
BF16 = jnp.bfloat16

D_MODEL = 1024
FOX_HEADS = 8
FOX_HD = 64
FOX_W = 512
SGU_GROUPS = 4
SGU_CH = 128
SGU_W = 512
CHUNK = 128
DIFF_HEADS = 8
DIFF_HD = 64
DIFF_W = 1024
N_EXPERTS = 8
PAGE_SIZE = 128
NORM_EPS = 1e-6
FOX_SCALE = FOX_HD ** -0.5
DIFF_SCALE = DIFF_HD ** -0.5
LAM_INIT = 0.8 - 0.6 * math.exp(-0.3 * 1)
LOG2E = math.log2(math.e)

LANES = 128
MXU_DIM = 256
VMEM_LIMIT = 52 * 1024 * 1024


def _cp(n_axes, vmem=VMEM_LIMIT):
    return pltpu.CompilerParams(dimension_semantics=("arbitrary",) * n_axes, vmem_limit_bytes=vmem)


def _dot(a, b):
    return jnp.dot(a, b, preferred_element_type=F32)


def _dot_nt(a, b):
    return lax.dot_general(a, b, (((1,), (1,)), ((), ())), preferred_element_type=F32)


def _dot_split3(x, m_bf16):
    hi = x.astype(BF16)
    r1 = x - hi.astype(F32)
    mid = r1.astype(BF16)
    lo = (r1 - mid.astype(F32)).astype(BF16)
    return _dot(hi, m_bf16) + _dot(mid, m_bf16) + _dot(lo, m_bf16)


def _rms(x):
    return x * lax.rsqrt(jnp.mean(x * x, axis=-1, keepdims=True) + NORM_EPS)


def _gelu(x):
    return 0.5 * x * (1.0 + lax.erf(x * (2.0 ** -0.5)))


def _seg_rms(z, seg_ref, gain):
    outs = []
    for j in range(z.shape[1] // MXU_DIM):
        zj = z[:, j * MXU_DIM:(j + 1) * MXU_DIM]
        ms = _dot((zj * zj).astype(BF16), seg_ref[...])
        outs.append(zj * lax.rsqrt(ms + NORM_EPS))
    return jnp.concatenate(outs, axis=1) * gain


def _row_tile(t):
    return 512 if t % 512 == 0 else t


def _even_in_kernel(x_ref, g_ref, wq, wk, wv, wf, wu, wvg, bq, bk, bv, bf, bu, bvg, qg, kg, lng, lnb, seg,
                    q_o, kt_o, vt_o, lft_o, u_o, vn_o, *extra, sample):
    xn = (_rms(x_ref[...]) * g_ref[...]).astype(BF16)

    def proj(w, b):
        return _dot(xn, w[...]) + b[...]

    q = _seg_rms(proj(wq, bq), seg, qg[...])
    k = _seg_rms(proj(wk, bk), seg, kg[...])
    kt_o[0] = k.T
    v = proj(wv, bv)
    v_t = v.T
    vt_o[0] = v_t
    lf = jax.nn.log_sigmoid(proj(wf, bf))
    lft_o[0] = lf.T[:FOX_HEADS]
    if sample:
        k_o, v_o, lf_o = extra
        q_o[...] = (q * FOX_SCALE).astype(BF16)
        k_o[...] = k
        v_o[...] = v
        lf_o[...] = lf[:, :FOX_HEADS]
    else:
        kb_o, vtb_o = extra
        q_o[0, 0] = (q * (FOX_SCALE * LOG2E)).T.astype(BF16)
        kb_o[...] = k.astype(BF16)
        vtb_o[0, 0] = v_t.astype(BF16)
    u_o[...] = _gelu(proj(wu, bu)).astype(BF16)
    vg = _gelu(proj(wvg, bvg))
    parts = []
    for gi in range(SGU_GROUPS):
        c = vg[:, gi * SGU_CH:(gi + 1) * SGU_CH]
        xc = c - jnp.mean(c, axis=-1, keepdims=True)
        parts.append(xc * lax.rsqrt(jnp.mean(xc * xc, axis=-1, keepdims=True) + NORM_EPS))
    vn = jnp.concatenate(parts, axis=1) * lng[...] + lnb[...]
    vn_o[...] = vn.astype(vn_o.dtype)


def _even_in(x, p, batch, seq, sample):
    t = x.shape[0]
    tm = _row_tile(seq)
    n_s = seq // tm
    row = lambda w: pl.BlockSpec((tm, w), lambda i: (i, 0))
    col = lambda w: pl.BlockSpec((1, w, tm), lambda i: (i // n_s, 0, i % n_s))
    full = lambda a: pl.BlockSpec(a.shape, lambda i: (0,) * a.ndim)
    ws = [p["wq"], p["wk"], p["wv"], p["wf"], p["wu"], p["wvg"], p["bq"], p["bk"], p["bv"], p["bf"], p["bu"],
          p["bvg"], p["qg"], p["kg"], p["lng"], p["lnb"], p["seg"]]
    nat = lambda w, dt: jax.ShapeDtypeStruct((t, w), dt)
    tr = lambda w: jax.ShapeDtypeStruct((batch, w, seq), F32)
    slab = pl.BlockSpec((1, 1, FOX_W, tm), lambda i: (i // n_s, i % n_s, 0, 0))
    slab_shape = jax.ShapeDtypeStruct((batch, n_s, FOX_W, tm), BF16)
    out_specs = [row(FOX_W) if sample else slab, col(FOX_W), col(FOX_W), col(FOX_HEADS), row(SGU_W), row(SGU_W)]
    out_shape = [nat(FOX_W, BF16) if sample else slab_shape, tr(FOX_W), tr(FOX_W), tr(FOX_HEADS),
                 nat(SGU_W, BF16), nat(SGU_W, F32 if sample else BF16)]
    if sample:
        out_specs += [row(FOX_W), row(FOX_W), row(FOX_HEADS)]
        out_shape += [nat(FOX_W, F32), nat(FOX_W, F32), nat(FOX_HEADS, F32)]
    else:
        out_specs += [row(FOX_W), slab]
        out_shape += [nat(FOX_W, BF16), slab_shape]
    return pl.pallas_call(
        functools.partial(_even_in_kernel, sample=sample),
        grid=(t // tm,),
        in_specs=[row(D_MODEL), full(p["g_mix"])] + [full(a) for a in ws],
        out_specs=out_specs,
        out_shape=out_shape,
        compiler_params=_cp(1),
        name="even_in_proj",
    )(x, p["g_mix"], *ws)


def _cumsum_kernel(x_ref, o_ref):
    r = lax.broadcasted_iota(jnp.int32, (LANES, LANES), 0)
    c = lax.broadcasted_iota(jnp.int32, (LANES, LANES), 1)
    upper = jnp.where(r <= c, 1.0, 0.0).astype(BF16)
    carry = jnp.zeros((x_ref.shape[0], 1), F32)
    for j in range(x_ref.shape[1] // LANES):
        blk = x_ref[:, j * LANES:(j + 1) * LANES]
        out = _dot_split3(blk, upper) + carry
        o_ref[:, j * LANES:(j + 1) * LANES] = out
        carry = out[:, LANES - 1:LANES]


def _cumsum_rows(x):
    return pl.pallas_call(
        _cumsum_kernel,
        out_shape=jax.ShapeDtypeStruct(x.shape, F32),
        name="logf_cumsum",
    )(x)


ATTN_COLS = 512


def _pair_attn_kernel(*refs, fox, tq):
    if fox:
        qt_ref, k_ref, vt_ref, c_ref, o_ref, s_buf, cb_s = refs
    else:
        qt_ref, k_ref, vt_ref, lam_ref, gain_ref, o_ref, s_buf = refs
    qi = pl.program_id(2)
    n_k = vt_ref.shape[1]
    qt = qt_ref[0, 0].astype(F32)
    rowid = lax.broadcasted_iota(jnp.int32, (LANES, 1), 0)
    n_col = tq // ATTN_COLS
    qt_cols = [qt[:, j * ATTN_COLS:(j + 1) * ATTN_COLS] for j in range(n_col)]
    qt_maps = [[jnp.where(rowid < FOX_HD, c, 0.0).astype(BF16) for c in qt_cols],
               [jnp.where(rowid >= FOX_HD, c, 0.0).astype(BF16) for c in qt_cols]]
    key = lax.broadcasted_iota(jnp.int32, (tq, ATTN_COLS), 0)
    qry = lax.broadcasted_iota(jnp.int32, (tq, ATTN_COLS), 1)

    if fox:
        @pl.when(qi == 0)
        def _():
            for h in range(2):
                for kj in range(n_k):
                    cb_s[h, kj] = _lane_bcast_cols(c_ref[0, 0, h, kj:kj + 1, :] * LOG2E)

    def scores(kj, slot):
        start = pl.multiple_of(kj * tq, tq)
        k_t = k_ref[pl.ds(start, tq), :]
        for h in range(2):
            for j in range(n_col):
                s = _dot(k_t, qt_maps[h][j])
                if fox:
                    s = s - jnp.tile(cb_s[h, kj], (1, ATTN_COLS // LANES))
                s_buf[slot, h * n_col + j] = s

    def update(kj, slot, carry, masked):
        vt_t = vt_ref[0, kj]
        new = []
        for h in range(2):
            v_rows = vt_t[h * FOX_HD:(h + 1) * FOX_HD] if fox else vt_t
            for j in range(n_col):
                m, l, acc = carry[h * n_col + j]
                s = s_buf[slot, h * n_col + j]
                if masked:
                    s = jnp.where(key <= qry + j * ATTN_COLS, s, -jnp.inf)
                m_new = jnp.maximum(m, jnp.max(s, axis=0, keepdims=True))
                alpha = jnp.exp2(m - m_new)
                p = jnp.exp2(s - m_new)
                l = alpha * l + jnp.sum(p, axis=0, keepdims=True)
                acc = alpha * acc + _dot(v_rows, p.astype(BF16))
                new.append((m_new, l, acc))
        return tuple(new)

    def two_tiles(t, carry):
        kj = 2 * t
        scores(kj + 1, 1)
        carry = update(kj, 0, carry, False)
        scores(kj + 2, 0)
        return update(kj + 1, 1, carry, False)

    def even_tail(carry):
        return update(qi, 0, carry, True)

    def odd_tail(carry):
        scores(qi, 1)
        carry = update(qi - 1, 0, carry, False)
        return update(qi, 1, carry, True)

    d_v = FOX_HD if fox else LANES
    init = tuple((jnp.full((1, ATTN_COLS), -jnp.inf, F32), jnp.zeros((1, ATTN_COLS), F32),
                  jnp.zeros((d_v, ATTN_COLS), F32)) for _ in range(2 * n_col))
    scores(0, 0)
    carry = lax.fori_loop(0, qi // 2, two_tiles, init)
    carry = lax.cond(qi % 2 == 0, even_tail, odd_tail, carry)
    o_a, o_b = (jnp.concatenate([acc * (1.0 / l) for _, l, acc in carry[h * n_col:(h + 1) * n_col]], axis=1)
                for h in range(2))
    if fox:
        o_ref[...] = jnp.concatenate([o_a, o_b], axis=0).T.astype(o_ref.dtype)
    else:
        lam = _diff_lambda(lam_ref)
        o = (o_a - lam * o_b).T
        o = _rms(o) * gain_ref[...] * (1.0 - LAM_INIT)
        o_ref[...] = o.astype(o_ref.dtype)


def _diff_lambda(lam_ref):
    a = jnp.sum(lam_ref[0:1, :] * lam_ref[1:2, :], axis=1, keepdims=True)
    b = jnp.sum(lam_ref[2:3, :] * lam_ref[3:4, :], axis=1, keepdims=True)
    return jnp.exp(a) - jnp.exp(b) + LAM_INIT


def _pair_attn(qt, k, vt, extra, fox):
    batch, nq, width, tq = qt.shape
    seq = nq * tq
    n_pairs = width // LANES
    qt_spec = pl.BlockSpec((1, 1, LANES, tq), lambda b, h, i: (b, i, h, 0))
    k_spec = pl.BlockSpec((seq, LANES), lambda b, h, i: (b, h))
    vt_spec = pl.BlockSpec((1, nq, LANES, tq), lambda b, h, i: (b, 0, h, 0))
    o_spec = pl.BlockSpec((tq, LANES), lambda b, h, i: (b * nq + i, h))
    scratch = [pltpu.VMEM((2, 2 * (tq // ATTN_COLS), tq, ATTN_COLS), F32)]
    if fox:
        extra_specs = [pl.BlockSpec((1, 1, 2, nq, tq), lambda b, h, i: (b, h, 0, 0, 0))]
        scratch += [pltpu.VMEM((2, nq, tq, LANES), F32)]
    else:
        extra_specs = [pl.BlockSpec(a.shape, lambda b, h, i: (0, 0)) for a in extra]
    return pl.pallas_call(
        functools.partial(_pair_attn_kernel, fox=fox, tq=tq),
        grid=(batch, n_pairs, nq),
        in_specs=[qt_spec, k_spec, vt_spec] + extra_specs,
        out_specs=o_spec,
        out_shape=jax.ShapeDtypeStruct(k.shape, BF16),
        scratch_shapes=scratch,
        compiler_params=_cp(3),
        name="fox_prompt_attn" if fox else "diff_prompt_attn",
    )(qt, k, vt, *extra)


def _even_out_kernel(x_ref, fo_ref, u_ref, vn_ref, ws_ref, bs_ref, w1, w2, o_ref, *, single):
    tm = x_ref.shape[0]
    u = u_ref[...].astype(F32)
    if single:
        so = u * (vn_ref[...].astype(F32) * ws_ref[...] + bs_ref[...])
    else:
        r = lax.broadcasted_iota(jnp.int32, (CHUNK, CHUNK), 0)
        c = lax.broadcasted_iota(jnp.int32, (CHUNK, CHUNK), 1)
        tril = [jnp.where(r >= c, ws_ref[g], 0.0).astype(BF16) for g in range(SGU_GROUPS)]
        rows = []
        for ch in range(tm // CHUNK):
            rs = slice(ch * CHUNK, (ch + 1) * CHUNK)
            cols = []
            for g in range(SGU_GROUPS):
                cs = slice(g * SGU_CH, (g + 1) * SGU_CH)
                mixed = _dot(tril[g], vn_ref[rs, cs]) + bs_ref[g]
                cols.append(u[rs, cs] * mixed)
            rows.append(jnp.concatenate(cols, axis=1))
        so = jnp.concatenate(rows, axis=0)
    o_ref[...] = x_ref[...] + _dot(fo_ref[...], w1[...]) + _dot(so.astype(BF16), w2[...])


def _even_out(x, fo, u, vn, ws, bs, w1, w2, single):
    t = x.shape[0]
    tm = _row_tile(t)
    row = lambda w: pl.BlockSpec((tm, w), lambda i: (i, 0))
    full = lambda a: pl.BlockSpec(a.shape, lambda i: (0,) * a.ndim)
    return pl.pallas_call(
        functools.partial(_even_out_kernel, single=single),
        grid=(t // tm,),
        in_specs=[row(D_MODEL), row(FOX_W), row(SGU_W), row(SGU_W), full(ws), full(bs), full(w1), full(w2)],
        out_specs=row(D_MODEL),
        out_shape=jax.ShapeDtypeStruct((t, D_MODEL), F32),
        compiler_params=_cp(1),
        name="even_out_proj",
    )(x, fo, u, vn, ws, bs, w1, w2)


def _swiglu_kernel(x_ref, g_ref, wg, wu, wd, o_ref, xn_s, acc_s):
    f = pl.program_id(1)

    @pl.when(f == 0)
    def _():
        xn_s[...] = (_rms(x_ref[...]) * g_ref[...]).astype(BF16)
        acc_s[...] = jnp.zeros_like(acc_s)

    xn = xn_s[...]
    a = _dot(xn, wg[...])
    b = _dot(xn, wu[...])
    h = (a * jax.nn.sigmoid(a) * b).astype(BF16)
    acc_s[...] += _dot(h, wd[...])

    @pl.when(f == pl.num_programs(1) - 1)
    def _():
        o_ref[...] = x_ref[...] + acc_s[...]


def _swiglu(x, g, wg, wu, wd, tf):
    t = x.shape[0]
    tm = _row_tile(t)
    d_ff = wg.shape[1]
    return pl.pallas_call(
        _swiglu_kernel,
        grid=(t // tm, d_ff // tf),
        in_specs=[pl.BlockSpec((tm, D_MODEL), lambda i, f: (i, 0)),
                  pl.BlockSpec((1, D_MODEL), lambda i, f: (0, 0)),
                  pl.BlockSpec((D_MODEL, tf), lambda i, f: (0, f)),
                  pl.BlockSpec((D_MODEL, tf), lambda i, f: (0, f)),
                  pl.BlockSpec((tf, D_MODEL), lambda i, f: (f, 0))],
        out_specs=pl.BlockSpec((tm, D_MODEL), lambda i, f: (i, 0)),
        out_shape=jax.ShapeDtypeStruct((t, D_MODEL), F32),
        scratch_shapes=[pltpu.VMEM((tm, D_MODEL), BF16), pltpu.VMEM((tm, D_MODEL), F32)],
        compiler_params=_cp(2),
        name="dense_swiglu",
    )(x, g, wg, wu, wd)


def _odd_in_kernel(x_ref, g_ref, wq, wk, wv, qg, kg, seg, q_o, kt_o, v_o, *extra, sample):
    xn = (_rms(x_ref[...]) * g_ref[...]).astype(BF16)
    q = _seg_rms(_dot(xn, wq[...]), seg, qg[...])
    k = _seg_rms(_dot(xn, wk[...]), seg, kg[...])
    kt_o[0] = k.T
    v = _dot(xn, wv[...])
    v_o[...] = v
    if sample:
        (k_o,) = extra
        q_o[...] = (q * DIFF_SCALE).astype(BF16)
        k_o[...] = k
    else:
        kb_o, vtb_o = extra
        q_o[0, 0] = (q * (DIFF_SCALE * LOG2E)).T.astype(BF16)
        kb_o[...] = k.astype(BF16)
        vtb_o[0, 0] = v.T.astype(BF16)


def _odd_in(x, p, batch, seq, sample):
    t = x.shape[0]
    tm = _row_tile(seq)
    n_s = seq // tm
    row = pl.BlockSpec((tm, DIFF_W), lambda i: (i, 0))
    col = pl.BlockSpec((1, DIFF_W, tm), lambda i: (i // n_s, 0, i % n_s))
    full = lambda a: pl.BlockSpec(a.shape, lambda i: (0,) * a.ndim)
    ws = [p["g_mix"], p["wq"], p["wk"], p["wv"], p["qg"], p["kg"], p["seg"]]
    nat = lambda dt: jax.ShapeDtypeStruct((t, DIFF_W), dt)
    slab = pl.BlockSpec((1, 1, DIFF_W, tm), lambda i: (i // n_s, i % n_s, 0, 0))
    slab_shape = jax.ShapeDtypeStruct((batch, n_s, DIFF_W, tm), BF16)
    out_specs = [row if sample else slab, col, row]
    out_shape = [nat(BF16) if sample else slab_shape, jax.ShapeDtypeStruct((batch, DIFF_W, seq), F32), nat(F32)]
    if sample:
        out_specs += [row]
        out_shape += [nat(F32)]
    else:
        out_specs += [row, slab]
        out_shape += [nat(BF16), slab_shape]
    return pl.pallas_call(
        functools.partial(_odd_in_kernel, sample=sample),
        grid=(t // tm,),
        in_specs=[row] + [full(a) for a in ws],
        out_specs=out_specs,
        out_shape=out_shape,
        compiler_params=_cp(1),
        name="odd_in_proj",
    )(x, *ws)


def _res_proj_kernel(x_ref, a_ref, w_ref, o_ref):
    o_ref[...] = x_ref[...] + _dot(a_ref[...], w_ref[...])


def _res_proj(x, a, w):
    t = x.shape[0]
    tm = _row_tile(t)
    return pl.pallas_call(
        _res_proj_kernel,
        grid=(t // tm,),
        in_specs=[pl.BlockSpec((tm, D_MODEL), lambda i: (i, 0)),
                  pl.BlockSpec((tm, a.shape[1]), lambda i: (i, 0)),
                  pl.BlockSpec(w.shape, lambda i: (0, 0))],
        out_specs=pl.BlockSpec((tm, D_MODEL), lambda i: (i, 0)),
        out_shape=jax.ShapeDtypeStruct((t, D_MODEL), F32),
        compiler_params=_cp(1),
        name="odd_out_proj",
    )(x, a, w)


def _router_kernel(x_ref, g_ref, w_ref, b_ref, xn_o, comb_o):
    xn = _rms(x_ref[...]) * g_ref[...]
    xn_o[...] = xn.astype(BF16)
    w = w_ref[...]
    w_hi = w.astype(BF16)
    w_lo = (w - w_hi.astype(F32)).astype(BF16)
    x_hi = xn.astype(BF16)
    x_lo = (xn - x_hi.astype(F32)).astype(BF16)
    logits = _dot(x_hi, w_hi) + _dot(x_hi, w_lo) + _dot(x_lo, w_hi) + b_ref[...]
    lane = lax.broadcasted_iota(jnp.int32, logits.shape, 1)
    logits = jnp.where(lane < N_EXPERTS, logits, -jnp.inf)
    v1 = jnp.max(logits, axis=1, keepdims=True)
    i1 = jnp.min(jnp.where(logits == v1, lane, LANES), axis=1, keepdims=True)
    rest = jnp.where(lane == i1, -jnp.inf, logits)
    v2 = jnp.max(rest, axis=1, keepdims=True)
    i2 = jnp.min(jnp.where(rest == v2, lane, LANES), axis=1, keepdims=True)
    e2 = jnp.exp(v2 - v1)
    g1 = 1.0 / (1.0 + e2)
    g2 = e2 * g1
    comb_o[...] = jnp.where(lane == i1, g1, 0.0) + jnp.where(lane == i2, g2, 0.0)


def _router(x, g, w, b):
    t = x.shape[0]
    tm = _row_tile(t)
    return pl.pallas_call(
        _router_kernel,
        grid=(t // tm,),
        in_specs=[pl.BlockSpec((tm, D_MODEL), lambda i: (i, 0)),
                  pl.BlockSpec((1, D_MODEL), lambda i: (0, 0)),
                  pl.BlockSpec((D_MODEL, LANES), lambda i: (0, 0)),
                  pl.BlockSpec((1, LANES), lambda i: (0, 0))],
        out_specs=[pl.BlockSpec((tm, D_MODEL), lambda i: (i, 0)), pl.BlockSpec((tm, LANES), lambda i: (i, 0))],
        out_shape=[jax.ShapeDtypeStruct((t, D_MODEL), BF16), jax.ShapeDtypeStruct((t, LANES), F32)],
        compiler_params=_cp(1),
        name="moe_router",
    )(x, g, w, b)


def _moe_kernel(y_ref, xn_ref, comb_ref, wg, wu, wd, o_ref, acc_s):
    e = pl.program_id(1)
    f = pl.program_id(2)

    @pl.when((e == 0) & (f == 0))
    def _():
        acc_s[...] = jnp.zeros_like(acc_s)

    xn = xn_ref[...]
    a = _dot(xn, wg[0])
    b = _dot(xn, wu[0])
    h = (a * jax.nn.sigmoid(a) * b).astype(BF16)
    comb = comb_ref[...]
    lane = lax.broadcasted_iota(jnp.int32, comb.shape, 1)
    gate = jnp.sum(jnp.where(lane == e, comb, 0.0), axis=1, keepdims=True)
    acc_s[...] += gate * _dot(h, wd[0])

    @pl.when((e == pl.num_programs(1) - 1) & (f == pl.num_programs(2) - 1))
    def _():
        o_ref[...] = y_ref[...] + acc_s[...]


def _moe(y, xn, comb, wg, wu, wd, tf):
    t = y.shape[0]
    tm = _row_tile(t)
    n_e, _, d_ff = wg.shape
    return pl.pallas_call(
        _moe_kernel,
        grid=(t // tm, n_e, d_ff // tf),
        in_specs=[pl.BlockSpec((tm, D_MODEL), lambda i, e, f: (i, 0)),
                  pl.BlockSpec((tm, D_MODEL), lambda i, e, f: (i, 0)),
                  pl.BlockSpec((tm, LANES), lambda i, e, f: (i, 0)),
                  pl.BlockSpec((1, D_MODEL, tf), lambda i, e, f: (e, 0, f)),
                  pl.BlockSpec((1, D_MODEL, tf), lambda i, e, f: (e, 0, f)),
                  pl.BlockSpec((1, tf, D_MODEL), lambda i, e, f: (e, f, 0))],
        out_specs=pl.BlockSpec((tm, D_MODEL), lambda i, e, f: (i, 0)),
        out_shape=jax.ShapeDtypeStruct((t, D_MODEL), F32),
        scratch_shapes=[pltpu.VMEM((tm, D_MODEL), F32)],
        compiler_params=_cp(3),
        name="moe_experts",
    )(y, xn, comb, wg, wu, wd)


MOE_TILE = 512
MOE_CHUNK = 512
MOE_WIN = 256
MOE_ALIGN = 16
GATHER_DMA_THREAD = 1


def _route_kernel(x_ref, g_ref, w_ref, b_ref, xn_o, cols_o, rows_o, cnt_o, carry_s):
    @pl.when(pl.program_id(0) == 0)
    def _():
        carry_s[...] = jnp.zeros_like(carry_s)

    xn = _rms(x_ref[...]) * g_ref[...]
    xn_o[...] = xn.astype(BF16)
    w = w_ref[...]
    w_hi = w.astype(BF16)
    w_lo = (w - w_hi.astype(F32)).astype(BF16)
    x_hi = xn.astype(BF16)
    x_lo = (xn - x_hi.astype(F32)).astype(BF16)
    logits = _dot(x_hi, w_hi) + _dot(x_hi, w_lo) + _dot(x_lo, w_hi) + b_ref[...]
    tm = logits.shape[0]
    lane = lax.broadcasted_iota(jnp.int32, logits.shape, 1)
    logits = jnp.where(lane < N_EXPERTS, logits, -jnp.inf)
    v1 = jnp.max(logits, axis=1, keepdims=True)
    i1 = jnp.min(jnp.where(logits == v1, lane, LANES), axis=1, keepdims=True)
    rest = jnp.where(lane == i1, -jnp.inf, logits)
    v2 = jnp.max(rest, axis=1, keepdims=True)
    i2 = jnp.min(jnp.where(rest == v2, lane, LANES), axis=1, keepdims=True)
    e2 = jnp.exp(v2 - v1)
    g1 = 1.0 / (1.0 + e2)
    g2 = e2 * g1
    chosen = jnp.where(lane == i1, 1.0, jnp.where(lane == i2, 1.0, 0.0))
    r = lax.broadcasted_iota(jnp.int32, (tm, tm), 0)
    c = lax.broadcasted_iota(jnp.int32, (tm, tm), 1)
    earlier = jnp.where(r > c, 1.0, 0.0).astype(BF16)
    ranks = _dot(earlier, chosen.astype(BF16)) + carry_s[...]
    tile_cnt = jnp.sum(chosen, axis=0, keepdims=True)
    carry_s[...] = carry_s[...] + tile_cnt
    cnt_o[0] = tile_cnt
    rank1 = jnp.sum(jnp.where(lane == i1, ranks, 0.0), axis=1, keepdims=True)
    rank2 = jnp.sum(jnp.where(lane == i2, ranks, 0.0), axis=1, keepdims=True)
    fields = (i1.astype(F32), i2.astype(F32), rank1, rank2, g1, g2)
    cols = jnp.zeros(logits.shape, F32)
    for k, v in enumerate(fields):
        cols = jnp.where(lane == k, v, cols)
    cols_o[...] = cols
    rows_o[0] = cols.T[:8]


def _sorted_rows(expert, rank, off_ref):
    off = jnp.zeros_like(rank)
    for e in range(N_EXPERTS):
        off = jnp.where(expert == e, off_ref[e].astype(F32), off)
    return rank + off


def _route(x, g, w, b):
    t = x.shape[0]
    tm = MOE_CHUNK
    n_c = t // tm
    return pl.pallas_call(
        _route_kernel,
        grid=(n_c,),
        in_specs=[pl.BlockSpec((tm, D_MODEL), lambda i: (i, 0)),
                  pl.BlockSpec((1, D_MODEL), lambda i: (0, 0)),
                  pl.BlockSpec((D_MODEL, LANES), lambda i: (0, 0)),
                  pl.BlockSpec((1, LANES), lambda i: (0, 0))],
        out_specs=[pl.BlockSpec((tm, D_MODEL), lambda i: (i, 0)),
                   pl.BlockSpec((tm, LANES), lambda i: (i, 0)),
                   pl.BlockSpec((1, 8, tm), lambda i: (i, 0, 0)),
                   pl.BlockSpec((1, 1, LANES), lambda i: (i, 0, 0))],
        out_shape=[jax.ShapeDtypeStruct((t, D_MODEL), BF16), jax.ShapeDtypeStruct((t, LANES), F32),
                   jax.ShapeDtypeStruct((n_c, 8, tm), F32), jax.ShapeDtypeStruct((n_c, 1, LANES), F32)],
        scratch_shapes=[pltpu.VMEM((1, LANES), F32)],
        compiler_params=_cp(1),
        name="moe_route",
    )(x, g, w, b)


def _moe_tile_kernel(se_ref, valid_ref, clo_ref, chi_ref, off_ref,
                     xn_hbm, rows_ref, wg, wu, wd, o_ref, xs_s, xg_s, gate_s, acc_s, cbuf, sem):
    j = pl.program_id(0)
    f = pl.program_id(1)
    n_f = pl.num_programs(1)
    valid = valid_ref[j] == 1

    def chunk_copy(c, slot):
        start = pl.multiple_of(c * MOE_CHUNK, MOE_CHUNK)
        return pltpu.make_async_copy(xn_hbm.at[pl.ds(start, MOE_CHUNK), :], cbuf.at[slot], sem.at[slot])

    @pl.when(valid & (f == 0))
    def _():
        c_lo = clo_ref[j]
        c_hi = chi_ref[j]
        target = (j * MOE_TILE + lax.broadcasted_iota(jnp.int32, (MOE_TILE, 1), 0)).astype(F32)
        xg_s[...] = jnp.zeros_like(xg_s)
        gate_s[...] = jnp.zeros_like(gate_s)
        chunk_copy(c_lo, 0).start(priority=GATHER_DMA_THREAD)

        def body(c, carry):
            slot = (c - c_lo) & 1
            chunk_copy(c, slot).wait()

            @pl.when(c < c_hi)
            def _():
                chunk_copy(c + 1, 1 - slot).start(priority=GATHER_DMA_THREAD)

            info = rows_ref[c]
            hit1 = _sorted_rows(info[0:1, :], info[2:3, :], off_ref) == target
            hit2 = _sorted_rows(info[1:2, :], info[3:4, :], off_ref) == target
            pick = jnp.where(hit1, 1.0, jnp.where(hit2, 1.0, 0.0)).astype(BF16)
            xg_s[...] += _dot(pick, cbuf[slot])
            gate_s[...] += jnp.sum(jnp.where(hit1, info[4:5, :], jnp.where(hit2, info[5:6, :], 0.0)),
                                   axis=1, keepdims=True)
            return carry

        lax.fori_loop(c_lo, c_hi + 1, body, 0)
        xs_s[...] = xg_s[...].astype(BF16)
        acc_s[...] = jnp.zeros_like(acc_s)

    @pl.when(valid)
    def _():
        xs = xs_s[...]
        a = _dot(xs, wg[0])
        b = _dot(xs, wu[0])
        h = (a * jax.nn.sigmoid(a) * b).astype(BF16)
        acc_s[...] += _dot(h, wd[0])

    @pl.when(valid & (f == n_f - 1))
    def _():
        o_ref[...] = (acc_s[...] * gate_s[...]).astype(o_ref.dtype)

    @pl.when(jnp.logical_not(valid) & (f == n_f - 1))
    def _():
        o_ref[...] = jnp.zeros_like(o_ref)


def _moe_tiles(sched, xn, rows, wg, wu, wd, tf):
    n_sched = sched[0].shape[0]
    n_e, _, d_ff = wg.shape
    n_f = d_ff // tf
    w_f = lambda f, ok: f * ok + (n_f - 1) * (1 - ok)
    w_in = pl.BlockSpec((1, D_MODEL, tf), lambda j, f, se, ok, *_: (se[j], 0, w_f(f, ok[j])))
    w_out = pl.BlockSpec((1, tf, D_MODEL), lambda j, f, se, ok, *_: (se[j], w_f(f, ok[j]), 0))
    n_rows = n_sched * MOE_TILE
    return pl.pallas_call(
        _moe_tile_kernel,
        grid_spec=pltpu.PrefetchScalarGridSpec(
            num_scalar_prefetch=5, grid=(n_sched, d_ff // tf),
            in_specs=[pl.BlockSpec(memory_space=pltpu.HBM),
                      pl.BlockSpec(rows.shape, lambda j, f, *_: (0, 0, 0)),
                      w_in, w_in, w_out],
            out_specs=pl.BlockSpec((MOE_TILE, D_MODEL), lambda j, f, *_: (j, 0)),
            scratch_shapes=[pltpu.VMEM((MOE_TILE, D_MODEL), BF16), pltpu.VMEM((MOE_TILE, D_MODEL), F32),
                            pltpu.VMEM((MOE_TILE, 1), F32), pltpu.VMEM((MOE_TILE, D_MODEL), F32),
                            pltpu.VMEM((2, MOE_CHUNK, D_MODEL), BF16), pltpu.SemaphoreType.DMA((2,))]),
        out_shape=jax.ShapeDtypeStruct((n_rows, D_MODEL), BF16),
        compiler_params=_cp(2, 60 * 1024 * 1024),
        name="moe_expert_tiles",
    )(*sched, xn, rows, wg, wu, wd)


def _moe_combine_kernel(start_ref, cnt_ref, limit_ref, off_ref, y_ref, cols_ref, yo_hbm, o_ref, acc_s, wbuf, sem):
    c = pl.program_id(0)
    cols = cols_ref[...]
    row1 = jnp.broadcast_to(_sorted_rows(cols[:, 0:1], cols[:, 2:3], off_ref), (MOE_CHUNK, MOE_WIN))
    row2 = jnp.broadcast_to(_sorted_rows(cols[:, 1:2], cols[:, 3:4], off_ref), (MOE_CHUNK, MOE_WIN))
    lane = lax.broadcasted_iota(jnp.int32, (1, MOE_WIN), 1).astype(F32)
    acc_s[...] = y_ref[...]

    def window(e, k):
        start = start_ref[c * N_EXPERTS + e]
        lo = ((start >> 4) << 4) + k * MOE_WIN
        begin = pl.multiple_of(jnp.minimum(lo, limit_ref[e] - MOE_WIN), MOE_ALIGN)
        copy = pltpu.make_async_copy(yo_hbm.at[pl.ds(begin, MOE_WIN), :], wbuf.at[e], sem.at[e])
        return copy, lo, begin

    def accumulate(e, lo, begin):
        pos = begin.astype(F32) + lane
        fresh = pos >= lo.astype(F32)
        pick = jnp.where(fresh, jnp.where(row1 == pos, 1.0, jnp.where(row2 == pos, 1.0, 0.0)), 0.0)
        acc_s[...] += _dot(pick.astype(BF16), wbuf[e])

    for e in range(N_EXPERTS):
        @pl.when(cnt_ref[c * N_EXPERTS + e] > 0)
        def _():
            window(e, 0)[0].start(priority=GATHER_DMA_THREAD)

    for e in range(N_EXPERTS):
        cnt = cnt_ref[c * N_EXPERTS + e]

        @pl.when(cnt > 0)
        def _():
            copy, lo, begin = window(e, 0)
            copy.wait()
            accumulate(e, lo, begin)
            start = start_ref[c * N_EXPERTS + e]
            n_win = (start + cnt - ((start >> 4) << 4) + MOE_WIN - 1) // MOE_WIN

            def more(k, carry):
                copy, lo, begin = window(e, k)
                copy.start()
                copy.wait()
                accumulate(e, lo, begin)
                return carry

            lax.fori_loop(1, n_win, more, 0)

    o_ref[...] = acc_s[...]


def _moe_combine(tables, y, cols, yo):
    t = y.shape[0]
    return pl.pallas_call(
        _moe_combine_kernel,
        grid_spec=pltpu.PrefetchScalarGridSpec(
            num_scalar_prefetch=4, grid=(t // MOE_CHUNK,),
            in_specs=[pl.BlockSpec((MOE_CHUNK, D_MODEL), lambda c, *_: (c, 0)),
                      pl.BlockSpec((MOE_CHUNK, LANES), lambda c, *_: (c, 0)),
                      pl.BlockSpec(memory_space=pltpu.HBM)],
            out_specs=pl.BlockSpec((MOE_CHUNK, D_MODEL), lambda c, *_: (c, 0)),
            scratch_shapes=[pltpu.VMEM((MOE_CHUNK, D_MODEL), F32),
                            pltpu.VMEM((N_EXPERTS, MOE_WIN, D_MODEL), BF16),
                            pltpu.SemaphoreType.DMA((N_EXPERTS,))]),
        out_shape=jax.ShapeDtypeStruct((t, D_MODEL), F32),
        compiler_params=_cp(1),
        name="moe_combine",
    )(*tables, y, cols, yo)


def _moe_top2(y, g, r_w, r_b, wg, wu, wd, tf):
    t = y.shape[0]
    n_chunks = t // MOE_CHUNK
    n_sched = 2 * t // MOE_TILE + N_EXPERTS
    xn, cols, rows, cnt_tiles = _route(y, g, r_w, r_b)

    cnt = cnt_tiles[:, 0, :N_EXPERTS].astype(jnp.int32)
    cum = jnp.concatenate([jnp.zeros((1, N_EXPERTS), jnp.int32), jnp.cumsum(cnt, axis=0)], axis=0)
    total = cum[-1]
    n_tiles = (total + MOE_TILE - 1) // MOE_TILE
    tile_end = jnp.cumsum(n_tiles)
    n_active = tile_end[-1]
    j = jnp.arange(n_sched, dtype=jnp.int32)
    valid = j < n_active
    se = jnp.sum(j[:, None] >= tile_end[None, :], axis=1).astype(jnp.int32)
    last_e = jnp.sum(n_active - 1 >= tile_end).astype(jnp.int32)
    se = jnp.where(valid, se, last_e)
    sm = j - (tile_end - n_tiles)[se]
    first = sm * MOE_TILE
    last = jnp.minimum(first + MOE_TILE, total[se]) - 1
    ends = cum[1:, :].T[se]
    c_lo = jnp.sum(ends <= first[:, None], axis=1).astype(jnp.int32)
    c_hi = jnp.sum(ends <= last[:, None], axis=1).astype(jnp.int32)
    c_lo = jnp.where(valid, c_lo, 0)
    c_hi = jnp.where(valid, jnp.minimum(c_hi, n_chunks - 1), 0)
    off = ((tile_end - n_tiles) * MOE_TILE).astype(jnp.int32)
    sched = (se, valid.astype(jnp.int32), c_lo, c_hi, off)
    yo = _moe_tiles(sched, xn, rows, wg, wu, wd, tf)

    starts = (off[None, :] + cum[:-1, :]).reshape(-1).astype(jnp.int32)
    limits = (off + n_tiles * MOE_TILE).astype(jnp.int32)
    return _moe_combine((starts, cnt.reshape(-1), limits, off), y, cols, yo)


def _fox_sample_kernel(pt_ref, q_ref, kn_ref, vn_ref, lfn_ref, *refs, n_pages):
    k_refs = refs[:n_pages]
    v_refs = refs[n_pages:2 * n_pages]
    lf_refs = refs[2 * n_pages:3 * n_pages]
    o_ref = refs[3 * n_pages]
    q = q_ref[0].astype(F32)
    qb = _lane_bcast_cols(q)
    sub = lax.broadcasted_iota(jnp.int32, (FOX_HEADS, FOX_W), 0)
    lane = lax.broadcasted_iota(jnp.int32, (FOX_HEADS, FOX_W), 1)
    sel = (lane >> 6) == sub
    r = lax.broadcasted_iota(jnp.int32, (PAGE_SIZE, PAGE_SIZE), 0)
    c = lax.broadcasted_iota(jnp.int32, (PAGE_SIZE, PAGE_SIZE), 1)
    later = jnp.where(r > c, 1.0, 0.0).astype(BF16)
    lf_new = lfn_ref[0]
    tail = jnp.zeros((FOX_HEADS, 1), F32)
    scores = [None] * n_pages
    for p in reversed(range(n_pages)):
        lf = lf_refs[p][0]
        bias = _dot_split3(lf, later) + tail + lf_new
        tail = tail + jnp.sum(lf, axis=1, keepdims=True)
        scores[p] = _slab_scores(k_refs[p], qb, range(FOX_HEADS)) + bias
    s_new = jnp.sum(jnp.where(sel, q * kn_ref[0], 0.0), axis=1, keepdims=True)
    m = s_new
    for s in scores:
        m = jnp.maximum(m, jnp.max(s, axis=1, keepdims=True))
    p_new = jnp.exp(s_new - m)
    l = p_new
    probs = [jnp.exp(s - m) for s in scores]
    for pr in probs:
        l = l + jnp.sum(pr, axis=1, keepdims=True)
    accs = []
    for h in range(FOX_HEADS):
        acc = jnp.zeros((FOX_HD, PAGE_SIZE), F32)
        for p in range(n_pages):
            acc = acc + v_refs[p][0, h] * probs[p][h:h + 1, :]
        accs.append(acc)
    o_row = jnp.sum(jnp.concatenate(accs, axis=0).T, axis=0, keepdims=True)
    inv_row = jnp.sum(jnp.where(sel, 1.0 / l, 0.0), axis=0, keepdims=True)
    new_row = jnp.sum(jnp.where(sel, p_new, 0.0), axis=0, keepdims=True)
    o_ref[0] = ((o_row + new_row * vn_ref[0]) * inv_row).astype(o_ref.dtype)


def _lane_bcast_cols(row):
    return jnp.broadcast_to(row, (LANES, row.shape[1])).T


def _slab_scores(kt_ref, qb, order):
    rows = []
    for r in order:
        prod = kt_ref[0, r] * qb[r * FOX_HD:(r + 1) * FOX_HD]
        rows.append(jnp.sum(prod, axis=0, keepdims=True))
    return jnp.concatenate(rows, axis=0)


def _fox_sample(page_table, q, k_new, v_new, lf_new, k_pool, v_pool, lf_pool_t):
    n_seq, n_pages = page_table.shape
    vec = lambda w: pl.BlockSpec((1, 1, w), lambda b, pt: (b, 0, 0))
    page = lambda shape, p: pl.BlockSpec((1,) + shape, lambda b, pt, p=p: (pt[b, p],) + (0,) * len(shape))
    in_specs = [vec(FOX_W), vec(FOX_W), vec(FOX_W), pl.BlockSpec((1, FOX_HEADS, 1), lambda b, pt: (b, 0, 0))]
    in_specs += [page((FOX_HEADS, FOX_HD, PAGE_SIZE), p) for p in range(n_pages)]
    in_specs += [page((FOX_HEADS, FOX_HD, PAGE_SIZE), p) for p in range(n_pages)]
    in_specs += [page((FOX_HEADS, PAGE_SIZE), p) for p in range(n_pages)]
    return pl.pallas_call(
        functools.partial(_fox_sample_kernel, n_pages=n_pages),
        grid_spec=pltpu.PrefetchScalarGridSpec(
            num_scalar_prefetch=1, grid=(n_seq,), in_specs=in_specs, out_specs=vec(FOX_W)),
        out_shape=jax.ShapeDtypeStruct((n_seq, 1, FOX_W), BF16),
        compiler_params=_cp(1),
        name="fox_sample_attn",
    )(page_table, q, k_new, v_new, lf_new, *([k_pool] * n_pages), *([v_pool] * n_pages),
      *([lf_pool_t] * n_pages))


def _diff_sample_kernel(pt_ref, q_ref, kn_ref, vn_ref, lam_ref, gain_ref, rep_ref, *refs, n_pages):
    k_refs = refs[:n_pages]
    v_refs = refs[n_pages:2 * n_pages]
    o_ref = refs[2 * n_pages]
    n_maps = 2 * DIFF_HEADS
    q = q_ref[0].astype(F32)
    qb = _lane_bcast_cols(q)
    sub = lax.broadcasted_iota(jnp.int32, (n_maps, DIFF_W), 0)
    lane = lax.broadcasted_iota(jnp.int32, (n_maps, DIFF_W), 1)
    sel = (lane >> 6) == ((sub & (DIFF_HEADS - 1)) * 2 + (sub >> 3))
    order = [2 * h + c for c in range(2) for h in range(DIFF_HEADS)]
    scores = [_slab_scores(k_refs[p], qb, order) for p in range(n_pages)]
    s_new = jnp.sum(jnp.where(sel, q * kn_ref[0], 0.0), axis=1, keepdims=True)
    m = s_new
    for s in scores:
        m = jnp.maximum(m, jnp.max(s, axis=1, keepdims=True))
    probs = [jnp.exp(s - m) for s in scores]
    p_new = jnp.exp(s_new - m)
    l = p_new
    for pr in probs:
        l = l + jnp.sum(pr, axis=1, keepdims=True)
    inv_l = 1.0 / l
    lam = _diff_lambda(lam_ref)

    def mix(pr):
        pn = pr * inv_l
        return pn[:DIFF_HEADS] - lam * pn[DIFF_HEADS:]

    sub8 = lax.broadcasted_iota(jnp.int32, (DIFF_HEADS, DIFF_HEADS * PAGE_SIZE), 0)
    lane8 = lax.broadcasted_iota(jnp.int32, (DIFF_HEADS, DIFF_HEADS * PAGE_SIZE), 1)
    own = (lane8 & (DIFF_HEADS - 1)) == sub8
    acc = mix(p_new) * vn_ref[0]
    for p in range(n_pages):
        spread = _dot(mix(probs[p]).astype(BF16), rep_ref[...])
        w_rows = jnp.where(own, spread, 0.0).astype(BF16)
        v_rows = v_refs[p][0].reshape(DIFF_HEADS * PAGE_SIZE, 2 * DIFF_HD).astype(BF16)
        acc = acc + _dot(w_rows, v_rows)
    o = _rms(acc) * gain_ref[...] * (1.0 - LAM_INIT)
    o_ref[0] = o.astype(o_ref.dtype)


def _diff_sample(page_table, q, k_new, v_new, lam_vecs, gain, k_pool, v_pool):
    n_seq, n_pages = page_table.shape
    vec = lambda w: pl.BlockSpec((1, 1, w), lambda b, pt: (b, 0, 0))
    head_rows = pl.BlockSpec((1, DIFF_HEADS, 2 * DIFF_HD), lambda b, pt: (b, 0, 0))
    page = lambda shape, p: pl.BlockSpec((1,) + shape, lambda b, pt, p=p: (pt[b, p], 0, 0, 0))
    i = jnp.arange(DIFF_HEADS * PAGE_SIZE)
    rep = (i[None, :] // DIFF_HEADS == jnp.arange(PAGE_SIZE)[:, None]).astype(BF16)
    in_specs = [vec(DIFF_W), vec(DIFF_W), head_rows,
                pl.BlockSpec(lam_vecs.shape, lambda b, pt: (0, 0)),
                pl.BlockSpec(gain.shape, lambda b, pt: (0, 0)),
                pl.BlockSpec(rep.shape, lambda b, pt: (0, 0))]
    in_specs += [page((2 * DIFF_HEADS, DIFF_HD, PAGE_SIZE), p) for p in range(n_pages)]
    in_specs += [page((PAGE_SIZE, DIFF_HEADS, 2 * DIFF_HD), p) for p in range(n_pages)]
    return pl.pallas_call(
        functools.partial(_diff_sample_kernel, n_pages=n_pages),
        grid_spec=pltpu.PrefetchScalarGridSpec(
            num_scalar_prefetch=1, grid=(n_seq,), in_specs=in_specs, out_specs=head_rows),
        out_shape=jax.ShapeDtypeStruct((n_seq, DIFF_HEADS, 2 * DIFF_HD), BF16),
        compiler_params=_cp(1, 60 * 1024 * 1024),
        name="diff_sample_attn",
    )(page_table, q, k_new, v_new, lam_vecs, gain, rep, *([k_pool] * n_pages), *([v_pool] * n_pages))


def _seg_matrix():
    i = jnp.arange(MXU_DIM)
    return jnp.where((i[:, None] // FOX_HD) == (i[None, :] // FOX_HD), 1.0 / FOX_HD, 0.0).astype(BF16)


def kernel(x_prompt, x_sample, cache_fox_k, cache_fox_v, cache_fox_logf, cache_diff_k, cache_diff_v, page_table,
           even_norm_mix, even_w_in, even_b_in, fox_q_gain, fox_k_gain, sgu_ln_gain, sgu_ln_bias, sgu_w_s, sgu_b_s,
           even_w_out, even_norm_ffn, ffn_w_gate, ffn_w_up, ffn_w_down,
           odd_norm_mix, odd_w_in, diff_q_gain, diff_k_gain, diff_lambda_q1, diff_lambda_k1, diff_lambda_q2,
           diff_lambda_k2, diff_subln_gain, odd_w_out, odd_norm_ffn, router_w, router_b,
           moe_w_gate, moe_w_up, moe_w_down):
    batch, seq, _ = x_prompt.shape
    n_seq = x_sample.shape[0]
    n_phys = cache_fox_k.shape[1]
    seg = _seg_matrix()
    row = lambda a: a.reshape(1, -1).astype(F32)

    w_in, b_in = even_w_in[0], even_b_in[0]
    o_f = 3 * FOX_W
    o_u = o_f + FOX_HEADS
    o_vg = o_u + SGU_W
    pad_f = LANES - FOX_HEADS
    pe = dict(
        g_mix=row(even_norm_mix[0]),
        wq=w_in[:, :FOX_W].astype(BF16), wk=w_in[:, FOX_W:2 * FOX_W].astype(BF16),
        wv=w_in[:, 2 * FOX_W:o_f].astype(BF16),
        wf=jnp.pad(w_in[:, o_f:o_u], ((0, 0), (0, pad_f))).astype(BF16),
        wu=w_in[:, o_u:o_vg].astype(BF16), wvg=w_in[:, o_vg:].astype(BF16),
        bq=row(b_in[:FOX_W]), bk=row(b_in[FOX_W:2 * FOX_W]), bv=row(b_in[2 * FOX_W:o_f]),
        bf=row(jnp.pad(b_in[o_f:o_u], (0, pad_f))), bu=row(b_in[o_u:o_vg]), bvg=row(b_in[o_vg:]),
        qg=row(jnp.tile(fox_q_gain[0], FOX_HEADS)), kg=row(jnp.tile(fox_k_gain[0], FOX_HEADS)),
        lng=row(sgu_ln_gain[0]), lnb=row(sgu_ln_bias[0]), seg=seg)
    w_s = sgu_w_s[0]
    b_s_full = jnp.broadcast_to(sgu_b_s[0][:, :, None], (SGU_GROUPS, CHUNK, SGU_CH)).astype(F32)
    w_s_first = row(jnp.repeat(w_s[:, 0, 0], SGU_CH))
    b_s_first = row(jnp.repeat(sgu_b_s[0][:, 0], SGU_CH))
    w_out1 = even_w_out[0][:FOX_W].astype(BF16)
    w_out2 = even_w_out[0][FOX_W:].astype(BF16)
    g_ffn = row(even_norm_ffn[0])
    ffn_g, ffn_u, ffn_d = ffn_w_gate[0].astype(BF16), ffn_w_up[0].astype(BF16), ffn_w_down[0].astype(BF16)

    wo_in = odd_w_in[0]
    po = dict(
        g_mix=row(odd_norm_mix[0]),
        wq=wo_in[:, :DIFF_W].astype(BF16), wk=wo_in[:, DIFF_W:2 * DIFF_W].astype(BF16),
        wv=wo_in[:, 2 * DIFF_W:].astype(BF16),
        qg=row(jnp.tile(diff_q_gain[0].reshape(-1), DIFF_HEADS)),
        kg=row(jnp.tile(diff_k_gain[0].reshape(-1), DIFF_HEADS)), seg=seg)
    lam_vecs = jnp.stack([diff_lambda_q1[0], diff_lambda_k1[0], diff_lambda_q2[0], diff_lambda_k2[0]]).astype(F32)
    subln = row(diff_subln_gain[0])
    subln_t = row(jnp.tile(diff_subln_gain[0], DIFF_HEADS))
    w_oo = odd_w_out[0].astype(BF16)
    g_moe = row(odd_norm_ffn[0])
    r_w = jnp.pad(router_w[0], ((0, 0), (0, LANES - N_EXPERTS))).astype(F32)
    r_b = row(jnp.pad(router_b[0], (0, LANES - N_EXPERTS)))
    moe_g, moe_u, moe_d = moe_w_gate[0].astype(BF16), moe_w_up[0].astype(BF16), moe_w_down[0].astype(BF16)

    xp = x_prompt.reshape(batch * seq, D_MODEL)
    xs = x_sample.reshape(n_seq, D_MODEL)

    def rows_out(a_t, *feat):
        b, _, s = a_t.shape
        n = len(feat)
        a = a_t.reshape((b,) + feat + (s,)).transpose((0, n + 1) + tuple(range(1, n + 1)))
        return a.reshape((1, b, s) + feat)

    qt, fkt_p, fvt_p, lft_p, u, vn, kb, vtb = _even_in(xp, pe, batch, seq, sample=False)
    tq = qt.shape[-1]
    c5 = _cumsum_rows(lft_p.reshape(batch * FOX_HEADS, seq)).reshape(batch, FOX_HEADS // 2, 2, seq // tq, tq)
    fo = _pair_attn(qt, kb, vtb, (c5,), fox=True)
    yp = _even_out(xp, fo, u, vn, w_s, b_s_full, w_out1, w_out2, single=False)
    yp = _swiglu(yp, g_ffn, ffn_g, ffn_u, ffn_d, tf=1408)

    qs, fkt_s, fvt_s, lft_s, us, vn_s, fk_s, fv_s, lf_s = _even_in(xs, pe, 1, n_seq, sample=True)
    fos = _fox_sample(
        page_table, qs.reshape(n_seq, 1, FOX_W), fk_s.reshape(n_seq, 1, FOX_W), fv_s.reshape(n_seq, 1, FOX_W),
        lf_s.reshape(n_seq, FOX_HEADS, 1),
        cache_fox_k[0].transpose(0, 2, 3, 1), cache_fox_v[0].transpose(0, 2, 3, 1),
        cache_fox_logf[0].transpose(0, 2, 1))
    ys = _even_out(xs, fos.reshape(n_seq, FOX_W), us, vn_s, w_s_first, b_s_first, w_out1, w_out2, single=True)
    ys = _swiglu(ys, g_ffn, ffn_g, ffn_u, ffn_d, tf=1408)

    qt, dkt_p, dv_p, kb, vtb = _odd_in(yp, po, batch, seq, sample=False)
    do = _pair_attn(qt, kb, vtb, (lam_vecs, subln), fox=False)
    yp = _res_proj(yp, do, w_oo)
    yp = _moe_top2(yp, g_moe, r_w, r_b, moe_g, moe_u, moe_d, tf=1792)

    qs, dkt_s, dv_s, dk_s = _odd_in(ys, po, 1, n_seq, sample=True)
    dos = _diff_sample(
        page_table, qs.reshape(n_seq, 1, DIFF_W), dk_s.reshape(n_seq, 1, DIFF_W),
        dv_s.reshape(n_seq, DIFF_HEADS, 2 * DIFF_HD), lam_vecs, subln,
        cache_diff_k[0].transpose(0, 2, 3, 4, 1).reshape(n_phys, 2 * DIFF_HEADS, DIFF_HD, PAGE_SIZE),
        cache_diff_v[0])
    ys = _res_proj(ys, dos.reshape(n_seq, DIFF_W), w_oo)
    xn, comb = _router(ys, g_moe, r_w, r_b)
    ys = _moe(ys, xn, comb, moe_g, moe_u, moe_d, tf=896)

    dec = x_sample.shape[1]
    sample_rows = lambda a_t, *feat: rows_out(a_t, *feat).reshape((1, n_seq, dec) + feat)
    return (yp.reshape(batch, seq, D_MODEL), ys.reshape(n_seq, dec, D_MODEL),
            rows_out(fkt_p, FOX_HEADS, FOX_HD), rows_out(fvt_p, FOX_HEADS, FOX_HD), rows_out(lft_p, FOX_HEADS),
            rows_out(dkt_p, DIFF_HEADS, 2, DIFF_HD), dv_p.reshape(1, batch, seq, DIFF_HEADS, 2 * DIFF_HD),
            sample_rows(fkt_s, FOX_HEADS, FOX_HD), sample_rows(fvt_s, FOX_HEADS, FOX_HD),
            sample_rows(lft_s, FOX_HEADS),
            sample_rows(dkt_s, DIFF_HEADS, 2, DIFF_HD), dv_s.reshape(1, n_seq, dec, DIFF_HEADS, 2 * DIFF_HD),
            vn_s.reshape(1, n_seq, dec, SGU_GROUPS, SGU_CH))
```
